```python
import jax, jax.numpy as jnp
from jax import lax
import numpy as np

D_MODEL = 1024
BATCH = 8
SEQ = 2048
DEPTH = 4

N_MIXERS = 2
N_CONV_LAYERS = (DEPTH + 1) // 2
N_SG_LAYERS = DEPTH // 2
CONV_WIDTH = 3
CONV_GROUPS = 16
SG_WIDTH = D_MODEL
SG_GROUPS = 8
SG_GROUP_DIM = SG_WIDTH // SG_GROUPS
CHUNK = 128
D_FF = int(-(-(8 * D_MODEL // 3) // 256) * 256) if (8 * D_MODEL) % 3 == 0 else ((8 * D_MODEL // 3) // 256 + 1) * 256
PLE_DIM = 256
RMS_EPS = 1e-6
LN_EPS = 1e-5

kernel_name = "hybrid_shortconv_gmlp_trunk"


def rms_norm(x, g):
    xf = x.astype(jnp.float32)
    var = jnp.mean(xf * xf, axis=-1, keepdims=True)
    return (xf * lax.rsqrt(var + RMS_EPS)).astype(x.dtype) * g


def layer_norm(x, g, b):
    xf = x.astype(jnp.float32)
    mu = jnp.mean(xf, axis=-1, keepdims=True)
    xc = xf - mu
    var = jnp.mean(xc * xc, axis=-1, keepdims=True)
    return (xc * lax.rsqrt(var + LN_EPS)).astype(x.dtype) * g + b


def causal_depthwise_conv(z, w_conv):
    return lax.conv_general_dilated(
        z, w_conv[:, None, :], window_strides=(1,), padding=[(CONV_WIDTH - 1, 0)],
        dimension_numbers=("NWC", "WIO", "NWC"), feature_group_count=z.shape[-1])


def short_conv_mixer(h, w_in, w_conv, w_out):
    bcx = h @ w_in
    b_gate, c_gate, xx = jnp.split(bcx, 3, axis=-1)
    y = causal_depthwise_conv(c_gate * xx, w_conv)
    return (b_gate * y) @ w_out


def spatial_gating_mixer(h, w_in, v_gain, v_bias, w_s, b_s, w_out):
    bsz, s, _ = h.shape
    u, v = jnp.split(h @ w_in, 2, axis=-1)
    v = layer_norm(v, v_gain, v_bias)
    n_chunks = s // CHUNK
    v = v.reshape(bsz, n_chunks, CHUNK, SG_GROUPS, SG_GROUP_DIM)
    causal = jnp.tril(jnp.ones((CHUNK, CHUNK), dtype=bool))
    w_masked = jnp.where(causal[None], w_s, jnp.zeros_like(w_s))
    mixed = jnp.einsum("gts,bcsgd->bctgd", w_masked, v) + b_s.T[:, :, None]
    y = u * mixed.reshape(bsz, s, SG_WIDTH)
    return y @ w_out


def swiglu_ffn(h, w_gate, w_up, w_down):
    return (jax.nn.silu(h @ w_gate) * (h @ w_up)) @ w_down


def setup_inputs(seed: int = 0) -> dict:
    key = jax.random.key(seed)
    ks = jax.random.split(key, 24)
    f32 = jnp.float32
    D = D_MODEL

    def nrm(k, shape, scale):
        return jax.random.normal(k, shape, f32) * scale

    def gain(k, shape):
        return 1.0 + 0.02 * jax.random.normal(k, shape, f32)

    causal = np.tril(np.ones((CHUNK, CHUNK), dtype=np.float32))
    return {
        "x": nrm(ks[0], (BATCH, SEQ, D), 1.0),
        "p": nrm(ks[1], (DEPTH, BATCH, SEQ, PLE_DIM), 1.0),
        "mix_norm": gain(ks[2], (DEPTH, D)),
        "conv_w_in": nrm(ks[3], (N_CONV_LAYERS, D, 3 * D), D ** -0.5),
        "conv_w": nrm(ks[4], (N_CONV_LAYERS, CONV_WIDTH, D), CONV_WIDTH ** -0.5),
        "conv_w_out": nrm(ks[5], (N_CONV_LAYERS, D, D), D ** -0.5),
        "sg_w_in": nrm(ks[6], (N_SG_LAYERS, D, 2 * SG_WIDTH), D ** -0.5),
        "sg_v_gain": gain(ks[7], (N_SG_LAYERS, SG_WIDTH)),
        "sg_v_bias": nrm(ks[8], (N_SG_LAYERS, SG_WIDTH), 0.02),
        "sg_w_spatial": nrm(ks[9], (N_SG_LAYERS, SG_GROUPS, CHUNK, CHUNK), 0.5 * CHUNK ** -0.5) * causal,
        "sg_b_spatial": gain(ks[10], (N_SG_LAYERS, SG_GROUPS, CHUNK)),
        "sg_w_out": nrm(ks[11], (N_SG_LAYERS, SG_WIDTH, D), SG_WIDTH ** -0.5),
        "ffn_norm": gain(ks[12], (DEPTH, D)),
        "ffn_w_gate": nrm(ks[13], (DEPTH, D, D_FF), D ** -0.5),
        "ffn_w_up": nrm(ks[14], (DEPTH, D, D_FF), D ** -0.5),
        "ffn_w_down": nrm(ks[15], (DEPTH, D_FF, D), D_FF ** -0.5),
        "ple_norm": gain(ks[16], (DEPTH, D)),
        "ple_w_gate": nrm(ks[17], (DEPTH, D, D), D ** -0.5),
        "ple_w_proj": nrm(ks[18], (DEPTH, PLE_DIM, D), 0.5 * PLE_DIM ** -0.5),
        "final_norm": gain(ks[19], (D,)),
    }


def reference(x, p, mix_norm, conv_w_in, conv_w, conv_w_out, sg_w_in, sg_v_gain, sg_v_bias,
              sg_w_spatial, sg_b_spatial, sg_w_out, ffn_norm, ffn_w_gate, ffn_w_up, ffn_w_down,
              ple_norm, ple_w_gate, ple_w_proj, final_norm):
    h = x
    for i in range(DEPTH):
        j = i // N_MIXERS
        hn = rms_norm(h, mix_norm[i])
        if i % N_MIXERS == 0:
            mix = short_conv_mixer(hn, conv_w_in[j], conv_w[j], conv_w_out[j])
        else:
            mix = spatial_gating_mixer(hn, sg_w_in[j], sg_v_gain[j], sg_v_bias[j],
                                       sg_w_spatial[j], sg_b_spatial[j], sg_w_out[j])
        h = h + mix
        h = h + swiglu_ffn(rms_norm(h, ffn_norm[i]), ffn_w_gate[i], ffn_w_up[i], ffn_w_down[i])
        gate = jax.nn.sigmoid(rms_norm(h, ple_norm[i]) @ ple_w_gate[i])
        h = h + gate * (p[i] @ ple_w_proj[i])
    return rms_norm(h, final_norm)
```

```python
import functools

import jax
import jax.numpy as jnp
from jax import lax
from jax.experimental import pallas as pl
from jax.experimental.pallas import tpu as pltpu

RMS_EPS = 1e-6
LN_EPS = 1e-5
CONV_WIDTH = 3
SUBLANES = 8
ROW_TILE = 512
VMEM_LIMIT_BYTES = 58 * 1024 * 1024

_MXU_DTYPE = jnp.bfloat16


def _rms(x, g):
    var = jnp.mean(x * x, axis=-1, keepdims=True)
    return (x * lax.rsqrt(var + RMS_EPS)) * g


def _dot(a, b):
    return jnp.dot(a, b, preferred_element_type=jnp.float32)


def _conv_mixer(hn, i, tiles_per_seq, w_in_ref, w_conv_ref, w_out_ref, carry_ref):
    tm, d = hn.shape
    bcx = _dot(hn, w_in_ref[...])
    b_gate = bcx[:, :d]
    z = bcx[:, d:2 * d] * bcx[:, 2 * d:]

    @pl.when(i % tiles_per_seq == 0)
    def _():
        carry_ref[...] = jnp.zeros_like(carry_ref)

    prev = carry_ref[...]
    carry_ref[...] = z[tm - SUBLANES:, :]

    w = w_conv_ref[...]
    w0, w1, w2 = w[0:1, :], w[1:2, :], w[2:3, :]
    y = w2 * z + w1 * pltpu.roll(z, 1, axis=0) + w0 * pltpu.roll(z, 2, axis=0)
    row = lax.broadcasted_iota(jnp.int32, (SUBLANES, d), 0)
    zh = z[:SUBLANES, :]
    z1h = jnp.where(row < 1, pltpu.roll(prev, 1, axis=0), pltpu.roll(zh, 1, axis=0))
    z2h = jnp.where(row < 2, pltpu.roll(prev, 2, axis=0), pltpu.roll(zh, 2, axis=0))
    yh = w2 * zh + w1 * z1h + w0 * z2h
    y = jnp.concatenate([yh, y[SUBLANES:, :]], axis=0)
    return _dot((b_gate * y).astype(_MXU_DTYPE), w_out_ref[...])


def _sg_mixer(hn, w_in_ref, gain_ref, bias_ref, w_s_ref, b_s_ref, w_out_ref, v_ref, y_ref):
    tm, d = hn.shape
    n_groups, chunk, _ = w_s_ref.shape
    gd = d // n_groups
    n_chunks = tm // chunk
    uv = _dot(hn, w_in_ref[...])
    u = uv[:, :d]
    v = uv[:, d:]
    mu = jnp.mean(v, axis=-1, keepdims=True)
    vc = v - mu
    var = jnp.mean(vc * vc, axis=-1, keepdims=True)
    v = (vc * lax.rsqrt(var + LN_EPS)) * gain_ref[...] + bias_ref[...]
    v_ref[...] = v.astype(_MXU_DTYPE)

    t_idx = lax.broadcasted_iota(jnp.int32, (chunk, chunk), 0)
    s_idx = lax.broadcasted_iota(jnp.int32, (chunk, chunk), 1)
    causal = s_idx <= t_idx
    b_s = b_s_ref[...]
    for g in range(n_groups):
        w_g = jnp.where(causal, w_s_ref[g], 0.0).astype(_MXU_DTYPE)
        cols = slice(g * gd, (g + 1) * gd)
        rhs = jnp.concatenate(
            [v_ref[c * chunk:(c + 1) * chunk, cols] for c in range(n_chunks)], axis=1)
        mixed = _dot(w_g, rhs)
        for c in range(n_chunks):
            rows = slice(c * chunk, (c + 1) * chunk)
            y_ref[rows, cols] = (
                u[rows, cols] * (mixed[:, c * gd:(c + 1) * gd] + b_s[:, cols])
            ).astype(_MXU_DTYPE)
    return _dot(y_ref[...], w_out_ref[...])


def _layer_kernel(*refs, mixer, is_last, tiles_per_seq):
    refs = list(refs)
    h_ref, p_ref, mix_norm_ref = refs[:3]
    n_mixer = {"conv": 3, "sg": 6}[mixer]
    mixer_refs = refs[3:3 + n_mixer]
    (ffn_norm_ref, w_gate_ref, w_up_ref, w_down_ref,
     ple_norm_ref, ple_w_gate_ref, ple_w_proj_ref) = refs[3 + n_mixer:10 + n_mixer]
    rest = refs[10 + n_mixer:]
    if is_last:
        final_norm_ref, rest = rest[0], rest[1:]
    o_ref, scratch = rest[0], rest[1:]

    i = pl.program_id(0)
    h = h_ref[...]
    hn = _rms(h, mix_norm_ref[...]).astype(_MXU_DTYPE)
    if mixer == "conv":
        h = h + _conv_mixer(hn, i, tiles_per_seq, *mixer_refs, *scratch)
    else:
        h = h + _sg_mixer(hn, *mixer_refs, *scratch)

    hn = _rms(h, ffn_norm_ref[...]).astype(_MXU_DTYPE)
    act = jax.nn.silu(_dot(hn, w_gate_ref[...])) * _dot(hn, w_up_ref[...])
    h = h + _dot(act.astype(_MXU_DTYPE), w_down_ref[...])

    hn = _rms(h, ple_norm_ref[...]).astype(_MXU_DTYPE)
    gate = jax.nn.sigmoid(_dot(hn, ple_w_gate_ref[...]))
    h = h + gate * _dot(p_ref[...].astype(_MXU_DTYPE), ple_w_proj_ref[...])
    if is_last:
        h = _rms(h, final_norm_ref[...])
    o_ref[...] = h


def _resident(arr, layer):
    tail = arr.shape[1:]
    zeros = (0,) * len(tail)
    return pl.BlockSpec((None,) + tail, lambda i: (layer,) + zeros,
                        pipeline_mode=pl.Buffered(1))


def _run_layer(h, p, layer, mixer, mixer_params, mixer_layer, shared_params, final_norm,
               seq_len, scratch_shapes):
    t, d = h.shape
    tm = ROW_TILE
    assert t % tm == 0 and seq_len % tm == 0
    is_last = final_norm is not None
    mix_norm, ffn_norm, w_gate, w_up, w_down, ple_norm, ple_w_gate, ple_w_proj = shared_params

    operands = [h, p, mix_norm]
    in_specs = [
        pl.BlockSpec((tm, d), lambda i: (i, 0)),
        pl.BlockSpec((None, tm, p.shape[-1]), lambda i: (layer, i, 0)),
        _resident(mix_norm, layer),
    ]
    for arr in mixer_params:
        operands.append(arr)
        in_specs.append(_resident(arr, mixer_layer))
    for arr in (ffn_norm, w_gate, w_up, w_down, ple_norm, ple_w_gate, ple_w_proj):
        operands.append(arr)
        in_specs.append(_resident(arr, layer))
    if is_last:
        operands.append(final_norm)
        in_specs.append(pl.BlockSpec(final_norm.shape, lambda i: (0, 0),
                                     pipeline_mode=pl.Buffered(1)))

    kernel = functools.partial(_layer_kernel, mixer=mixer, is_last=is_last,
                               tiles_per_seq=seq_len // tm)
    return pl.pallas_call(
        kernel,
        grid=(t // tm,),
        in_specs=in_specs,
        out_specs=pl.BlockSpec((tm, d), lambda i: (i, 0)),
        out_shape=jax.ShapeDtypeStruct((t, d), jnp.float32),
        scratch_shapes=scratch_shapes,
        compiler_params=pltpu.CompilerParams(
            dimension_semantics=("arbitrary",),
            vmem_limit_bytes=VMEM_LIMIT_BYTES),
        name=f"layer{layer}_{mixer}",
    )(*operands)


def kernel(x, p, mix_norm, conv_w_in, conv_w, conv_w_out, sg_w_in, sg_v_gain, sg_v_bias,
           sg_w_spatial, sg_b_spatial, sg_w_out, ffn_norm, ffn_w_gate, ffn_w_up, ffn_w_down,
           ple_norm, ple_w_gate, ple_w_proj, final_norm):
    bsz, seq_len, d = x.shape
    depth = p.shape[0]
    t = bsz * seq_len
    n_groups, chunk = sg_b_spatial.shape[1:]
    gd = d // n_groups
    assert ROW_TILE % chunk == 0

    bf = lambda a: a.astype(_MXU_DTYPE)
    vec = lambda a: a.reshape(a.shape[0], 1, a.shape[-1])
    shared = (vec(mix_norm), vec(ffn_norm), bf(ffn_w_gate), bf(ffn_w_up), bf(ffn_w_down),
              vec(ple_norm), bf(ple_w_gate), bf(ple_w_proj))
    conv_params = (bf(conv_w_in), conv_w, bf(conv_w_out))
    b_s_tile = jnp.repeat(jnp.swapaxes(sg_b_spatial, 1, 2), gd, axis=2)
    sg_params = (bf(sg_w_in), vec(sg_v_gain), vec(sg_v_bias), sg_w_spatial, b_s_tile,
                 bf(sg_w_out))

    h = x.reshape(t, d)
    p2 = p.reshape(depth, t, p.shape[-1])
    for layer in range(depth):
        last = layer == depth - 1
        fn = final_norm.reshape(1, d) if last else None
        if layer % 2 == 0:
            scratch = [pltpu.VMEM((SUBLANES, d), jnp.float32)]
            h = _run_layer(h, p2, layer, "conv", conv_params, layer // 2, shared, fn,
                           seq_len, scratch)
        else:
            scratch = [pltpu.VMEM((ROW_TILE, d), _MXU_DTYPE),
                       pltpu.VMEM((ROW_TILE, d), _MXU_DTYPE)]
            h = _run_layer(h, p2, layer, "sg", sg_params, layer // 2, shared, fn,
                           seq_len, scratch)
    return h.reshape(bsz, seq_len, d)
```

```python
import functools

import jax
import jax.numpy as jnp
from jax import lax
from jax.experimental import pallas as pl
from jax.experimental.pallas import tpu as pltpu

RMS_EPS = 1e-6
LN_EPS = 1e-5
SUBLANES = 8
BF16_ROWS = 16
ROW_TILE = 512
VMEM_LIMIT_BYTES = 58 * 1024 * 1024

_MXU_DTYPE = jnp.bfloat16
N_BIG = 7


def _rms(x, g):
    var = jnp.mean(x * x, axis=-1, keepdims=True)
    return (x * lax.rsqrt(var + RMS_EPS)) * g


def _dot(a, b):
    return jnp.dot(a, b, preferred_element_type=jnp.float32)


def _conv_mixer(hn, i, tiles_per_seq, w_in_ref, w_out_ref, w_conv_ref, carry_ref):
    tm, d = hn.shape
    bcx = _dot(hn, w_in_ref[...])
    b_gate = bcx[:, :d]
    z = bcx[:, d:2 * d] * bcx[:, 2 * d:]

    @pl.when(i % tiles_per_seq == 0)
    def _():
        carry_ref[...] = jnp.zeros_like(carry_ref)

    prev = carry_ref[...]
    carry_ref[...] = z[tm - SUBLANES:, :]

    w = w_conv_ref[...]
    w0, w1, w2 = w[0:1, :], w[1:2, :], w[2:3, :]
    y = w2 * z + w1 * pltpu.roll(z, 1, axis=0) + w0 * pltpu.roll(z, 2, axis=0)
    row = lax.broadcasted_iota(jnp.int32, (SUBLANES, d), 0)
    zh = z[:SUBLANES, :]
    z1h = jnp.where(row < 1, pltpu.roll(prev, 1, axis=0), pltpu.roll(zh, 1, axis=0))
    z2h = jnp.where(row < 2, pltpu.roll(prev, 2, axis=0), pltpu.roll(zh, 2, axis=0))
    yh = w2 * zh + w1 * z1h + w0 * z2h
    y = jnp.concatenate([yh, y[SUBLANES:, :]], axis=0)
    return _dot((b_gate * y).astype(_MXU_DTYPE), w_out_ref[...])


def _sg_mixer(hn, w_in_ref, w_out_ref, gain_ref, bias_ref, w_s_ref, b_s_ref, v_ref, y_ref):
    tm, d = hn.shape
    n_groups, chunk, _ = w_s_ref.shape
    gd = d // n_groups
    n_chunks = tm // chunk
    uv = _dot(hn, w_in_ref[...])
    u = uv[:, :d]
    v = uv[:, d:]
    mu = jnp.mean(v, axis=-1, keepdims=True)
    vc = v - mu
    var = jnp.mean(vc * vc, axis=-1, keepdims=True)
    v = (vc * lax.rsqrt(var + LN_EPS)) * gain_ref[...] + bias_ref[...]
    v_ref[...] = v.astype(_MXU_DTYPE)

    t_idx = lax.broadcasted_iota(jnp.int32, (chunk, chunk), 0)
    s_idx = lax.broadcasted_iota(jnp.int32, (chunk, chunk), 1)
    causal = s_idx <= t_idx
    b_s = b_s_ref[...]
    for g in range(n_groups):
        w_g = jnp.where(causal, w_s_ref[g], 0.0).astype(_MXU_DTYPE)
        cols = slice(g * gd, (g + 1) * gd)
        rhs = jnp.concatenate(
            [v_ref[c * chunk:(c + 1) * chunk, cols] for c in range(n_chunks)], axis=1)
        mixed = _dot(w_g, rhs)
        for c in range(n_chunks):
            rows = slice(c * chunk, (c + 1) * chunk)
            y_ref[rows, cols] = (
                u[rows, cols] * (mixed[:, c * gd:(c + 1) * gd] + b_s[:, cols])
            ).astype(_MXU_DTYPE)
    return _dot(y_ref[...], w_out_ref[...])


def _layer_kernel(*refs, mixer, n_small, is_last, n_next, tiles_per_seq):
    refs = list(refs)
    take = lambda n: [refs.pop(0) for _ in range(n)]
    h_ref, p_ref, mix_norm_ref, ffn_norm_ref, ple_norm_ref = take(5)
    (w_in_ref, w_out_ref, w_gate_ref, w_up_ref, w_down_ref,
     ple_w_gate_ref, ple_w_proj_ref) = take(N_BIG)
    small_refs = take(n_small)
    final_norm_ref = take(1)[0] if is_last else None
    next_f32_refs = take(n_next)
    o_ref = take(1)[0]
    next_bf16_refs = take(n_next)
    scratch = refs

    i = pl.program_id(0)
    h = h_ref[...]
    hn = _rms(h, mix_norm_ref[...]).astype(_MXU_DTYPE)
    if mixer == "conv":
        h = h + _conv_mixer(hn, i, tiles_per_seq, w_in_ref, w_out_ref, *small_refs, *scratch)
    else:
        h = h + _sg_mixer(hn, w_in_ref, w_out_ref, *small_refs, *scratch)

    hn = _rms(h, ffn_norm_ref[...]).astype(_MXU_DTYPE)
    act = jax.nn.silu(_dot(hn, w_gate_ref[...])) * _dot(hn, w_up_ref[...])
    h = h + _dot(act.astype(_MXU_DTYPE), w_down_ref[...])

    hn = _rms(h, ple_norm_ref[...]).astype(_MXU_DTYPE)
    gate = jax.nn.sigmoid(_dot(hn, ple_w_gate_ref[...]))
    h = h + gate * _dot(p_ref[...].astype(_MXU_DTYPE), ple_w_proj_ref[...])
    if is_last:
        h = _rms(h, final_norm_ref[...])
    o_ref[...] = h
    for src, dst in zip(next_f32_refs, next_bf16_refs):
        dst[...] = src[...].astype(_MXU_DTYPE)


def _const_spec(shape, index):
    return pl.BlockSpec(shape, lambda i: index, pipeline_mode=pl.Buffered(1))


def _stacked(arr, layer):
    tail = arr.shape[1:]
    return _const_spec((None,) + tail, (layer,) + (0,) * len(tail))


def _row_blocks(rows, n_steps):
    n = n_steps
    while rows % (n * BF16_ROWS):
        assert n % 2 == 0, (rows, n_steps)
        n //= 2
    return n


def _run_layer(h, p, layer, mixer, big, small, norms, final_norm, next_big, seq_len,
               scratch_shapes):
    t, d = h.shape
    tm = ROW_TILE
    assert t % tm == 0 and seq_len % tm == 0
    n_steps = t // tm
    is_last = final_norm is not None

    operands = [h, p]
    in_specs = [pl.BlockSpec((tm, d), lambda i: (i, 0)),
                pl.BlockSpec((None, tm, p.shape[-1]), lambda i: (layer, i, 0))]
    for arr in norms:
        operands.append(arr)
        in_specs.append(_stacked(arr, layer))
    for arr in big:
        operands.append(arr)
        in_specs.append(_const_spec(arr.shape, (0, 0)))
    for arr, idx in small:
        operands.append(arr)
        in_specs.append(_stacked(arr, idx))
    if is_last:
        operands.append(final_norm)
        in_specs.append(_const_spec(final_norm.shape, (0, 0)))

    out_shape = [jax.ShapeDtypeStruct((t, d), jnp.float32)]
    out_specs = [pl.BlockSpec((tm, d), lambda i: (i, 0))]
    for arr, idx in next_big:
        _, rows, cols = arr.shape
        n_blocks = _row_blocks(rows, n_steps)
        per = n_steps // n_blocks
        operands.append(arr)
        in_specs.append(pl.BlockSpec((None, rows // n_blocks, cols),
                                     lambda i, idx=idx, per=per: (idx, i // per, 0)))
        out_shape.append(jax.ShapeDtypeStruct((rows, cols), _MXU_DTYPE))
        out_specs.append(pl.BlockSpec((rows // n_blocks, cols),
                                      lambda i, per=per: (i // per, 0)))

    kernel = functools.partial(_layer_kernel, mixer=mixer, n_small=len(small), is_last=is_last,
                               n_next=len(next_big), tiles_per_seq=seq_len // tm)
    outs = pl.pallas_call(
        kernel,
        grid=(n_steps,),
        in_specs=in_specs,
        out_specs=out_specs,
        out_shape=out_shape,
        scratch_shapes=scratch_shapes,
        compiler_params=pltpu.CompilerParams(
            dimension_semantics=("arbitrary",),
            vmem_limit_bytes=VMEM_LIMIT_BYTES),
        name=f"layer{layer}_{mixer}",
    )(*operands)
    return outs[0], list(outs[1:])


def kernel(x, p, mix_norm, conv_w_in, conv_w, conv_w_out, sg_w_in, sg_v_gain, sg_v_bias,
           sg_w_spatial, sg_b_spatial, sg_w_out, ffn_norm, ffn_w_gate, ffn_w_up, ffn_w_down,
           ple_norm, ple_w_gate, ple_w_proj, final_norm):
    bsz, seq_len, d = x.shape
    depth = p.shape[0]
    t = bsz * seq_len
    n_groups, chunk = sg_b_spatial.shape[1:]
    gd = d // n_groups
    assert ROW_TILE % chunk == 0

    vec = lambda a: a.reshape(a.shape[0], 1, a.shape[-1])
    norms = (vec(mix_norm), vec(ffn_norm), vec(ple_norm))
    b_s_tile = jnp.repeat(jnp.swapaxes(sg_b_spatial, 1, 2), gd, axis=2)
    sg_gain, sg_bias = vec(sg_v_gain), vec(sg_v_bias)

    def big_f32(layer):
        j = layer // 2
        mix = [(conv_w_in, j), (conv_w_out, j)] if layer % 2 == 0 else [(sg_w_in, j), (sg_w_out, j)]
        return mix + [(w, layer) for w in (ffn_w_gate, ffn_w_up, ffn_w_down, ple_w_gate, ple_w_proj)]

    h = x.reshape(t, d)
    p2 = p.reshape(depth, t, p.shape[-1])
    big = [arr[idx].astype(_MXU_DTYPE) for arr, idx in big_f32(0)]
    for layer in range(depth):
        last = layer == depth - 1
        fn = final_norm.reshape(1, d) if last else None
        next_big = [] if last else big_f32(layer + 1)
        j = layer // 2
        if layer % 2 == 0:
            mixer, small = "conv", [(conv_w, j)]
            scratch = [pltpu.VMEM((SUBLANES, d), jnp.float32)]
        else:
            mixer = "sg"
            small = [(sg_gain, j), (sg_bias, j), (sg_w_spatial, j), (b_s_tile, j)]
            scratch = [pltpu.VMEM((ROW_TILE, d), _MXU_DTYPE),
                       pltpu.VMEM((ROW_TILE, d), _MXU_DTYPE)]
        h, big = _run_layer(h, p2, layer, mixer, big, small, norms, fn, next_big, seq_len,
                            scratch)
    return h.reshape(bsz, seq_len, d)
```

```python
import functools

import jax
import jax.numpy as jnp
from jax import lax
from jax.experimental import pallas as pl
from jax.experimental.pallas import tpu as pltpu

RMS_EPS = 1e-6
LN_EPS = 1e-5
SUBLANES = 8
BF16_ROWS = 16
ROW_TILE = 512
ROW_CHAINS = 2
VMEM_LIMIT_BYTES = 58 * 1024 * 1024

_MXU_DTYPE = jnp.bfloat16
N_BIG = 7


def _rms(x, g):
    var = jnp.mean(x * x, axis=-1, keepdims=True)
    return (x * lax.rsqrt(var + RMS_EPS)) * g


def _dot(a, b):
    return jnp.dot(a, b, preferred_element_type=jnp.float32)


def _conv_gate(bcx, prev, w_conv_ref):
    tm = bcx.shape[0]
    d = bcx.shape[1] // 3
    b_gate = bcx[:, :d]
    z = bcx[:, d:2 * d] * bcx[:, 2 * d:]

    w = w_conv_ref[...]
    w0, w1, w2 = w[0:1, :], w[1:2, :], w[2:3, :]
    y = w2 * z + w1 * pltpu.roll(z, 1, axis=0) + w0 * pltpu.roll(z, 2, axis=0)
    row = lax.broadcasted_iota(jnp.int32, (SUBLANES, d), 0)
    zh = z[:SUBLANES, :]
    z1h = jnp.where(row < 1, pltpu.roll(prev, 1, axis=0), pltpu.roll(zh, 1, axis=0))
    z2h = jnp.where(row < 2, pltpu.roll(prev, 2, axis=0), pltpu.roll(zh, 2, axis=0))
    yh = w2 * zh + w1 * z1h + w0 * z2h
    y = jnp.concatenate([yh, y[SUBLANES:, :]], axis=0)
    return (b_gate * y).astype(_MXU_DTYPE), z[tm - SUBLANES:, :]


def _sg_gate(uv, gain_ref, bias_ref, w_s_ref, b_s_ref, v_ref, y_ref):
    tm = uv.shape[0]
    d = uv.shape[1] // 2
    n_groups, chunk, _ = w_s_ref.shape
    gd = d // n_groups
    n_chunks = tm // chunk
    u = uv[:, :d]
    v = uv[:, d:]
    mu = jnp.mean(v, axis=-1, keepdims=True)
    vc = v - mu
    var = jnp.mean(vc * vc, axis=-1, keepdims=True)
    v = (vc * lax.rsqrt(var + LN_EPS)) * gain_ref[...] + bias_ref[...]
    v_ref[...] = v.astype(_MXU_DTYPE)

    t_idx = lax.broadcasted_iota(jnp.int32, (chunk, chunk), 0)
    s_idx = lax.broadcasted_iota(jnp.int32, (chunk, chunk), 1)
    causal = s_idx <= t_idx
    b_s = b_s_ref[...]
    for g in range(n_groups):
        w_g = jnp.where(causal, w_s_ref[g], 0.0).astype(_MXU_DTYPE)
        cols = slice(g * gd, (g + 1) * gd)
        rhs = jnp.concatenate(
            [v_ref[c * chunk:(c + 1) * chunk, cols] for c in range(n_chunks)], axis=1)
        mixed = _dot(w_g, rhs)
        for c in range(n_chunks):
            rows = slice(c * chunk, (c + 1) * chunk)
            y_ref[rows, cols] = (
                u[rows, cols] * (mixed[:, c * gd:(c + 1) * gd] + b_s[:, cols])
            ).astype(_MXU_DTYPE)
    return y_ref[...]


def _layer_kernel(*refs, mixer, n_small, is_last, n_next, tiles_per_seq):
    refs = list(refs)
    take = lambda n: [refs.pop(0) for _ in range(n)]
    h_ref, p_ref, mix_norm_ref, ffn_norm_ref, ple_norm_ref = take(5)
    (w_in_ref, w_out_ref, w_gate_ref, w_up_ref, w_down_ref,
     ple_w_gate_ref, ple_w_proj_ref) = take(N_BIG)
    small_refs = take(n_small)
    final_norm_ref = take(1)[0] if is_last else None
    next_f32_refs = take(n_next)
    o_ref = take(1)[0]
    next_bf16_refs = take(n_next)
    scratch = refs

    carry = {}
    if mixer == "conv":
        carry_ref, = scratch

        @pl.when(pl.program_id(0) % tiles_per_seq == 0)
        def _():
            carry_ref[...] = jnp.zeros_like(carry_ref)

        carry["prev"] = carry_ref[...]

    def row_block(rows):
        h = h_ref[rows, :]
        hn = _rms(h, mix_norm_ref[...]).astype(_MXU_DTYPE)
        pre = _dot(hn, w_in_ref[...])
        yield
        if mixer == "conv":
            gated, carry["prev"] = _conv_gate(pre, carry["prev"], *small_refs)
        else:
            gated = _sg_gate(pre, *small_refs, *(s.at[rows, :] for s in scratch))
        mix = _dot(gated, w_out_ref[...])
        yield
        h = h + mix
        hn = _rms(h, ffn_norm_ref[...]).astype(_MXU_DTYPE)
        ffn_gate = _dot(hn, w_gate_ref[...])
        ffn_up = _dot(hn, w_up_ref[...])
        yield
        act = (jax.nn.silu(ffn_gate) * ffn_up).astype(_MXU_DTYPE)
        ffn = _dot(act, w_down_ref[...])
        yield
        h = h + ffn
        hn = _rms(h, ple_norm_ref[...]).astype(_MXU_DTYPE)
        ple_gate = _dot(hn, ple_w_gate_ref[...])
        ple_proj = _dot(p_ref[rows, :].astype(_MXU_DTYPE), ple_w_proj_ref[...])
        yield
        h = h + jax.nn.sigmoid(ple_gate) * ple_proj
        if is_last:
            h = _rms(h, final_norm_ref[...])
        o_ref[rows, :] = h

    rb = h_ref.shape[0] // ROW_CHAINS
    blocks = [row_block(pl.ds(c * rb, rb)) for c in range(ROW_CHAINS)]
    while blocks:
        blocks = [b for b in blocks if next(b, StopIteration) is not StopIteration]

    if mixer == "conv":
        carry_ref[...] = carry["prev"]
    for src, dst in zip(next_f32_refs, next_bf16_refs):
        dst[...] = src[...].astype(_MXU_DTYPE)


def _const_spec(shape, index):
    return pl.BlockSpec(shape, lambda i: index, pipeline_mode=pl.Buffered(1))


def _stacked(arr, layer):
    tail = arr.shape[1:]
    return _const_spec((None,) + tail, (layer,) + (0,) * len(tail))


def _row_blocks(rows, n_steps):
    n = n_steps
    while rows % (n * BF16_ROWS):
        assert n % 2 == 0, (rows, n_steps)
        n //= 2
    return n


def _run_layer(h, p, layer, mixer, big, small, norms, final_norm, next_big, seq_len,
               scratch_shapes):
    t, d = h.shape
    tm = ROW_TILE
    assert t % tm == 0 and seq_len % tm == 0
    n_steps = t // tm
    is_last = final_norm is not None

    operands = [h, p]
    in_specs = [pl.BlockSpec((tm, d), lambda i: (i, 0)),
                pl.BlockSpec((None, tm, p.shape[-1]), lambda i: (layer, i, 0))]
    for arr in norms:
        operands.append(arr)
        in_specs.append(_stacked(arr, layer))
    for arr in big:
        operands.append(arr)
        in_specs.append(_const_spec(arr.shape, (0, 0)))
    for arr, idx in small:
        operands.append(arr)
        in_specs.append(_stacked(arr, idx))
    if is_last:
        operands.append(final_norm)
        in_specs.append(_const_spec(final_norm.shape, (0, 0)))

    out_shape = [jax.ShapeDtypeStruct((t, d), jnp.float32)]
    out_specs = [pl.BlockSpec((tm, d), lambda i: (i, 0))]
    for arr, idx in next_big:
        _, rows, cols = arr.shape
        n_blocks = _row_blocks(rows, n_steps)
        per = n_steps // n_blocks
        operands.append(arr)
        in_specs.append(pl.BlockSpec((None, rows // n_blocks, cols),
                                     lambda i, idx=idx, per=per: (idx, i // per, 0)))
        out_shape.append(jax.ShapeDtypeStruct((rows, cols), _MXU_DTYPE))
        out_specs.append(pl.BlockSpec((rows // n_blocks, cols),
                                      lambda i, per=per: (i // per, 0)))

    kernel = functools.partial(_layer_kernel, mixer=mixer, n_small=len(small), is_last=is_last,
                               n_next=len(next_big), tiles_per_seq=seq_len // tm)
    outs = pl.pallas_call(
        kernel,
        grid=(n_steps,),
        in_specs=in_specs,
        out_specs=out_specs,
        out_shape=out_shape,
        scratch_shapes=scratch_shapes,
        compiler_params=pltpu.CompilerParams(
            dimension_semantics=("arbitrary",),
            vmem_limit_bytes=VMEM_LIMIT_BYTES),
        name=f"layer{layer}_{mixer}",
    )(*operands)
    return outs[0], list(outs[1:])


def kernel(x, p, mix_norm, conv_w_in, conv_w, conv_w_out, sg_w_in, sg_v_gain, sg_v_bias,
           sg_w_spatial, sg_b_spatial, sg_w_out, ffn_norm, ffn_w_gate, ffn_w_up, ffn_w_down,
           ple_norm, ple_w_gate, ple_w_proj, final_norm):
    bsz, seq_len, d = x.shape
    depth = p.shape[0]
    t = bsz * seq_len
    n_groups, chunk = sg_b_spatial.shape[1:]
    gd = d // n_groups
    assert (ROW_TILE // ROW_CHAINS) % chunk == 0

    vec = lambda a: a.reshape(a.shape[0], 1, a.shape[-1])
    norms = (vec(mix_norm), vec(ffn_norm), vec(ple_norm))
    b_s_tile = jnp.repeat(jnp.swapaxes(sg_b_spatial, 1, 2), gd, axis=2)
    sg_gain, sg_bias = vec(sg_v_gain), vec(sg_v_bias)

    def big_f32(layer):
        j = layer // 2
        mix = [(conv_w_in, j), (conv_w_out, j)] if layer % 2 == 0 else [(sg_w_in, j), (sg_w_out, j)]
        return mix + [(w, layer) for w in (ffn_w_gate, ffn_w_up, ffn_w_down, ple_w_gate, ple_w_proj)]

    h = x.reshape(t, d)
    p2 = p.reshape(depth, t, p.shape[-1])
    big = [arr[idx].astype(_MXU_DTYPE) for arr, idx in big_f32(0)]
    for layer in range(depth):
        last = layer == depth - 1
        fn = final_norm.reshape(1, d) if last else None
        next_big = [] if last else big_f32(layer + 1)
        j = layer // 2
        if layer % 2 == 0:
            mixer, small = "conv", [(conv_w, j)]
            scratch = [pltpu.VMEM((SUBLANES, d), jnp.float32)]
        else:
            mixer = "sg"
            small = [(sg_gain, j), (sg_bias, j), (sg_w_spatial, j), (b_s_tile, j)]
            scratch = [pltpu.VMEM((ROW_TILE, d), _MXU_DTYPE),
                       pltpu.VMEM((ROW_TILE, d), _MXU_DTYPE)]
        h, big = _run_layer(h, p2, layer, mixer, big, small, norms, fn, next_big, seq_len,
                            scratch)
    return h.reshape(bsz, seq_len, d)
```

```python
import functools

import jax
import jax.numpy as jnp
from jax import lax
from jax.experimental import pallas as pl
from jax.experimental.pallas import tpu as pltpu

RMS_EPS = 1e-6
LN_EPS = 1e-5
SUBLANES = 8
BF16_ROWS = 16
ROW_TILE = 512
ROW_CHAINS = 2
FFN_COLS = 256
VMEM_LIMIT_BYTES = 58 * 1024 * 1024

_MXU_DTYPE = jnp.bfloat16
N_BIG = 7


def _rms(x, g):
    var = jnp.mean(x * x, axis=-1, keepdims=True)
    return (x * lax.rsqrt(var + RMS_EPS)) * g


def _dot(a, b):
    return jnp.dot(a, b, preferred_element_type=jnp.float32)


def _conv_gate(bcx, prev, w_conv_ref):
    tm = bcx.shape[0]
    d = bcx.shape[1] // 3
    b_gate = bcx[:, :d]
    z = bcx[:, d:2 * d] * bcx[:, 2 * d:]

    w = w_conv_ref[...]
    w0, w1, w2 = w[0:1, :], w[1:2, :], w[2:3, :]
    y = w2 * z + w1 * pltpu.roll(z, 1, axis=0) + w0 * pltpu.roll(z, 2, axis=0)
    row = lax.broadcasted_iota(jnp.int32, (SUBLANES, d), 0)
    zh = z[:SUBLANES, :]
    z1h = jnp.where(row < 1, pltpu.roll(prev, 1, axis=0), pltpu.roll(zh, 1, axis=0))
    z2h = jnp.where(row < 2, pltpu.roll(prev, 2, axis=0), pltpu.roll(zh, 2, axis=0))
    yh = w2 * zh + w1 * z1h + w0 * z2h
    y = jnp.concatenate([yh, y[SUBLANES:, :]], axis=0)
    return (b_gate * y).astype(_MXU_DTYPE), z[tm - SUBLANES:, :]


def _sg_gate(uv, gain_ref, bias_ref, w_s_ref, b_s_ref, v_ref, y_ref):
    tm = uv.shape[0]
    d = uv.shape[1] // 2
    n_groups, chunk, _ = w_s_ref.shape
    gd = d // n_groups
    n_chunks = tm // chunk
    u = uv[:, :d]
    v = uv[:, d:]
    mu = jnp.mean(v, axis=-1, keepdims=True)
    vc = v - mu
    var = jnp.mean(vc * vc, axis=-1, keepdims=True)
    v = (vc * lax.rsqrt(var + LN_EPS)) * gain_ref[...] + bias_ref[...]
    v_ref[...] = v.astype(_MXU_DTYPE)

    t_idx = lax.broadcasted_iota(jnp.int32, (chunk, chunk), 0)
    s_idx = lax.broadcasted_iota(jnp.int32, (chunk, chunk), 1)
    causal = s_idx <= t_idx
    b_s = b_s_ref[...]
    for g in range(n_groups):
        w_g = jnp.where(causal, w_s_ref[g], 0.0).astype(_MXU_DTYPE)
        cols = slice(g * gd, (g + 1) * gd)
        rhs = jnp.concatenate(
            [v_ref[c * chunk:(c + 1) * chunk, cols] for c in range(n_chunks)], axis=1)
        mixed = _dot(w_g, rhs)
        for c in range(n_chunks):
            rows = slice(c * chunk, (c + 1) * chunk)
            y_ref[rows, cols] = (
                u[rows, cols] * (mixed[:, c * gd:(c + 1) * gd] + b_s[:, cols])
            ).astype(_MXU_DTYPE)
    return y_ref[...]


def _layer_kernel(*refs, mixer, n_small, is_last, n_next, tiles_per_seq):
    refs = list(refs)
    take = lambda n: [refs.pop(0) for _ in range(n)]
    h_ref, p_ref, mix_norm_ref, ffn_norm_ref, ple_norm_ref = take(5)
    (w_in_ref, w_out_ref, w_gate_ref, w_up_ref, w_down_ref,
     ple_w_gate_ref, ple_w_proj_ref) = take(N_BIG)
    small_refs = take(n_small)
    final_norm_ref = take(1)[0] if is_last else None
    next_f32_refs = take(n_next)
    o_ref = take(1)[0]
    next_bf16_refs = take(n_next)
    scratch = refs

    carry = {}
    if mixer == "conv":
        carry_ref, = scratch

        @pl.when(pl.program_id(0) % tiles_per_seq == 0)
        def _():
            carry_ref[...] = jnp.zeros_like(carry_ref)

        carry["prev"] = carry_ref[...]

    def row_block(rows):
        h = h_ref[rows, :]
        hn = _rms(h, mix_norm_ref[...]).astype(_MXU_DTYPE)
        pre = _dot(hn, w_in_ref[...])
        yield
        if mixer == "conv":
            gated, carry["prev"] = _conv_gate(pre, carry["prev"], *small_refs)
        else:
            gated = _sg_gate(pre, *small_refs, *(s.at[rows, :] for s in scratch))
        mix = _dot(gated, w_out_ref[...])
        yield
        h = h + mix
        hn = _rms(h, ffn_norm_ref[...]).astype(_MXU_DTYPE)
        acts = []
        for n in range(w_gate_ref.shape[1] // FFN_COLS):
            cols = slice(n * FFN_COLS, (n + 1) * FFN_COLS)
            ffn_gate = _dot(hn, w_gate_ref[:, cols])
            ffn_up = _dot(hn, w_up_ref[:, cols])
            acts.append((jax.nn.silu(ffn_gate) * ffn_up).astype(_MXU_DTYPE))
        act = jnp.concatenate(acts, axis=1)
        yield
        ffn = _dot(act, w_down_ref[...])
        yield
        h = h + ffn
        hn = _rms(h, ple_norm_ref[...]).astype(_MXU_DTYPE)
        ple_gate = _dot(hn, ple_w_gate_ref[...])
        ple_proj = _dot(p_ref[rows, :].astype(_MXU_DTYPE), ple_w_proj_ref[...])
        yield
        h = h + jax.nn.sigmoid(ple_gate) * ple_proj
        if is_last:
            h = _rms(h, final_norm_ref[...])
        o_ref[rows, :] = h

    rb = h_ref.shape[0] // ROW_CHAINS
    blocks = [row_block(pl.ds(c * rb, rb)) for c in range(ROW_CHAINS)]
    while blocks:
        blocks = [b for b in blocks if next(b, StopIteration) is not StopIteration]

    if mixer == "conv":
        carry_ref[...] = carry["prev"]
    for src, dst in zip(next_f32_refs, next_bf16_refs):
        dst[...] = src[...].astype(_MXU_DTYPE)


def _const_spec(shape, index):
    return pl.BlockSpec(shape, lambda i: index, pipeline_mode=pl.Buffered(1))


def _stacked(arr, layer):
    tail = arr.shape[1:]
    return _const_spec((None,) + tail, (layer,) + (0,) * len(tail))


def _row_blocks(rows, n_steps):
    n = n_steps
    while rows % (n * BF16_ROWS):
        assert n % 2 == 0, (rows, n_steps)
        n //= 2
    return n


def _run_layer(h, p, layer, mixer, big, small, norms, final_norm, next_big, seq_len,
               scratch_shapes):
    t, d = h.shape
    tm = ROW_TILE
    assert t % tm == 0 and seq_len % tm == 0
    n_steps = t // tm
    is_last = final_norm is not None

    operands = [h, p]
    in_specs = [pl.BlockSpec((tm, d), lambda i: (i, 0)),
                pl.BlockSpec((None, tm, p.shape[-1]), lambda i: (layer, i, 0))]
    for arr in norms:
        operands.append(arr)
        in_specs.append(_stacked(arr, layer))
    for arr in big:
        operands.append(arr)
        in_specs.append(_const_spec(arr.shape, (0, 0)))
    for arr, idx in small:
        operands.append(arr)
        in_specs.append(_stacked(arr, idx))
    if is_last:
        operands.append(final_norm)
        in_specs.append(_const_spec(final_norm.shape, (0, 0)))

    out_shape = [jax.ShapeDtypeStruct((t, d), jnp.float32)]
    out_specs = [pl.BlockSpec((tm, d), lambda i: (i, 0))]
    for arr, idx in next_big:
        _, rows, cols = arr.shape
        n_blocks = _row_blocks(rows, n_steps)
        per = n_steps // n_blocks
        operands.append(arr)
        in_specs.append(pl.BlockSpec((None, rows // n_blocks, cols),
                                     lambda i, idx=idx, per=per: (idx, i // per, 0)))
        out_shape.append(jax.ShapeDtypeStruct((rows, cols), _MXU_DTYPE))
        out_specs.append(pl.BlockSpec((rows // n_blocks, cols),
                                      lambda i, per=per: (i // per, 0)))

    kernel = functools.partial(_layer_kernel, mixer=mixer, n_small=len(small), is_last=is_last,
                               n_next=len(next_big), tiles_per_seq=seq_len // tm)
    outs = pl.pallas_call(
        kernel,
        grid=(n_steps,),
        in_specs=in_specs,
        out_specs=out_specs,
        out_shape=out_shape,
        scratch_shapes=scratch_shapes,
        compiler_params=pltpu.CompilerParams(
            dimension_semantics=("arbitrary",),
            vmem_limit_bytes=VMEM_LIMIT_BYTES),
        name=f"layer{layer}_{mixer}",
    )(*operands)
    return outs[0], list(outs[1:])


def kernel(x, p, mix_norm, conv_w_in, conv_w, conv_w_out, sg_w_in, sg_v_gain, sg_v_bias,
           sg_w_spatial, sg_b_spatial, sg_w_out, ffn_norm, ffn_w_gate, ffn_w_up, ffn_w_down,
           ple_norm, ple_w_gate, ple_w_proj, final_norm):
    bsz, seq_len, d = x.shape
    depth = p.shape[0]
    t = bsz * seq_len
    n_groups, chunk = sg_b_spatial.shape[1:]
    gd = d // n_groups
    assert (ROW_TILE // ROW_CHAINS) % chunk == 0

    vec = lambda a: a.reshape(a.shape[0], 1, a.shape[-1])
    norms = (vec(mix_norm), vec(ffn_norm), vec(ple_norm))
    b_s_tile = jnp.repeat(jnp.swapaxes(sg_b_spatial, 1, 2), gd, axis=2)
    sg_gain, sg_bias = vec(sg_v_gain), vec(sg_v_bias)

    def big_f32(layer):
        j = layer // 2
        mix = [(conv_w_in, j), (conv_w_out, j)] if layer % 2 == 0 else [(sg_w_in, j), (sg_w_out, j)]
        return mix + [(w, layer) for w in (ffn_w_gate, ffn_w_up, ffn_w_down, ple_w_gate, ple_w_proj)]

    h = x.reshape(t, d)
    p2 = p.reshape(depth, t, p.shape[-1])
    big = [arr[idx].astype(_MXU_DTYPE) for arr, idx in big_f32(0)]
    for layer in range(depth):
        last = layer == depth - 1
        fn = final_norm.reshape(1, d) if last else None
        next_big = [] if last else big_f32(layer + 1)
        j = layer // 2
        if layer % 2 == 0:
            mixer, small = "conv", [(conv_w, j)]
            scratch = [pltpu.VMEM((SUBLANES, d), jnp.float32)]
        else:
            mixer = "sg"
            small = [(sg_gain, j), (sg_bias, j), (sg_w_spatial, j), (b_s_tile, j)]
            scratch = [pltpu.VMEM((ROW_TILE, d), _MXU_DTYPE),
                       pltpu.VMEM((ROW_TILE, d), _MXU_DTYPE)]
        h, big = _run_layer(h, p2, layer, mixer, big, small, norms, fn, next_big, seq_len,
                            scratch)
    return h.reshape(bsz, seq_len, d)
```

```python
import functools

import jax
import jax.numpy as jnp
from jax import lax
from jax.experimental import pallas as pl
from jax.experimental.pallas import tpu as pltpu

RMS_EPS = 1e-6
LN_EPS = 1e-5
SUBLANES = 8
BF16_ROWS = 16
ROW_TILE = 512
ROW_CHAINS = 2
FFN_COLS = 256
STAGE_ROWS, STAGE_COLS = 256, 1024
VMEM_LIMIT_BYTES = 58 * 1024 * 1024

_MXU_DTYPE = jnp.bfloat16
N_BIG = 7


def _rms(x, g):
    var = jnp.mean(x * x, axis=-1, keepdims=True)
    return (x * lax.rsqrt(var + RMS_EPS)) * g


def _dot(a, b):
    return jnp.dot(a, b, preferred_element_type=jnp.float32)


def _conv_gate(bcx, prev, w_conv_ref):
    tm = bcx.shape[0]
    d = bcx.shape[1] // 3
    b_gate = bcx[:, :d]
    z = bcx[:, d:2 * d] * bcx[:, 2 * d:]

    w = w_conv_ref[...]
    w0, w1, w2 = w[0:1, :], w[1:2, :], w[2:3, :]
    y = w2 * z + w1 * pltpu.roll(z, 1, axis=0) + w0 * pltpu.roll(z, 2, axis=0)
    row = lax.broadcasted_iota(jnp.int32, (SUBLANES, d), 0)
    zh = z[:SUBLANES, :]
    z1h = jnp.where(row < 1, pltpu.roll(prev, 1, axis=0), pltpu.roll(zh, 1, axis=0))
    z2h = jnp.where(row < 2, pltpu.roll(prev, 2, axis=0), pltpu.roll(zh, 2, axis=0))
    yh = w2 * zh + w1 * z1h + w0 * z2h
    y = jnp.concatenate([yh, y[SUBLANES:, :]], axis=0)
    return (b_gate * y).astype(_MXU_DTYPE), z[tm - SUBLANES:, :]


def _sg_gate(uv, gain_ref, bias_ref, w_s_ref, b_s_ref, v_ref, y_ref):
    tm = uv.shape[0]
    d = uv.shape[1] // 2
    n_groups, chunk, _ = w_s_ref.shape
    gd = d // n_groups
    n_chunks = tm // chunk
    u = uv[:, :d]
    v = uv[:, d:]
    mu = jnp.mean(v, axis=-1, keepdims=True)
    vc = v - mu
    var = jnp.mean(vc * vc, axis=-1, keepdims=True)
    v = (vc * lax.rsqrt(var + LN_EPS)) * gain_ref[...] + bias_ref[...]
    v_ref[...] = v.astype(_MXU_DTYPE)

    t_idx = lax.broadcasted_iota(jnp.int32, (chunk, chunk), 0)
    s_idx = lax.broadcasted_iota(jnp.int32, (chunk, chunk), 1)
    causal = s_idx <= t_idx
    b_s = b_s_ref[...]
    for g in range(n_groups):
        w_g = jnp.where(causal, w_s_ref[g], 0.0).astype(_MXU_DTYPE)
        cols = slice(g * gd, (g + 1) * gd)
        rhs = jnp.concatenate(
            [v_ref[c * chunk:(c + 1) * chunk, cols] for c in range(n_chunks)], axis=1)
        mixed = _dot(w_g, rhs)
        for c in range(n_chunks):
            rows = slice(c * chunk, (c + 1) * chunk)
            y_ref[rows, cols] = (
                u[rows, cols] * (mixed[:, c * gd:(c + 1) * gd] + b_s[:, cols])
            ).astype(_MXU_DTYPE)
    return y_ref[...]


def _round_weights_into_vmem(hbm_refs, layers, vmem_refs, stage_ref, sem_ref):
    chunks = []
    for w, dst in enumerate(vmem_refs):
        k, n = dst.shape
        assert k % STAGE_ROWS == 0
        chunks += [(w, r0, c0, min(STAGE_COLS, n - c0))
                   for r0 in range(0, k, STAGE_ROWS) for c0 in range(0, n, STAGE_COLS)]

    def copy(j):
        w, r0, c0, cols = chunks[j]
        return pltpu.make_async_copy(
            hbm_refs[w].at[layers[w], pl.ds(r0, STAGE_ROWS), pl.ds(c0, cols)],
            stage_ref.at[j % 2, :, pl.ds(0, cols)],
            sem_ref.at[j % 2])

    copy(0).start()
    for j, (w, r0, c0, cols) in enumerate(chunks):
        if j + 1 < len(chunks):
            copy(j + 1).start()
        copy(j).wait()
        vmem_refs[w][pl.ds(r0, STAGE_ROWS), pl.ds(c0, cols)] = (
            stage_ref[j % 2, :, pl.ds(0, cols)].astype(_MXU_DTYPE))


def _layer_kernel(*refs, mixer, n_small, is_last, n_next, tiles_per_seq, own_layers):
    refs = list(refs)
    take = lambda n: [refs.pop(0) for _ in range(n)]
    h_ref, p_ref, mix_norm_ref, ffn_norm_ref, ple_norm_ref = take(5)
    big_refs = take(N_BIG)
    small_refs = take(n_small)
    final_norm_ref = take(1)[0] if is_last else None
    next_f32_refs = take(n_next)
    o_ref = take(1)[0]
    next_bf16_refs = take(n_next)
    scratch = refs

    if own_layers is not None:
        w_vmem, (stage_ref, sem_ref) = scratch[-N_BIG - 2:-2], scratch[-2:]
        scratch = scratch[:-N_BIG - 2]

        @pl.when(pl.program_id(0) == 0)
        def _():
            _round_weights_into_vmem(big_refs, own_layers, w_vmem, stage_ref, sem_ref)

        big_refs = w_vmem
    (w_in_ref, w_out_ref, w_gate_ref, w_up_ref, w_down_ref,
     ple_w_gate_ref, ple_w_proj_ref) = big_refs

    carry = {}
    if mixer == "conv":
        carry_ref, = scratch

        @pl.when(pl.program_id(0) % tiles_per_seq == 0)
        def _():
            carry_ref[...] = jnp.zeros_like(carry_ref)

        carry["prev"] = carry_ref[...]

    def row_block(rows):
        h = h_ref[rows, :]
        hn = _rms(h, mix_norm_ref[...]).astype(_MXU_DTYPE)
        pre = _dot(hn, w_in_ref[...])
        yield
        if mixer == "conv":
            gated, carry["prev"] = _conv_gate(pre, carry["prev"], *small_refs)
        else:
            gated = _sg_gate(pre, *small_refs, *(s.at[rows, :] for s in scratch))
        mix = _dot(gated, w_out_ref[...])
        yield
        h = h + mix
        hn = _rms(h, ffn_norm_ref[...]).astype(_MXU_DTYPE)
        acts = []
        for n in range(w_gate_ref.shape[1] // FFN_COLS):
            cols = slice(n * FFN_COLS, (n + 1) * FFN_COLS)
            ffn_gate = _dot(hn, w_gate_ref[:, cols])
            ffn_up = _dot(hn, w_up_ref[:, cols])
            acts.append((jax.nn.silu(ffn_gate) * ffn_up).astype(_MXU_DTYPE))
        act = jnp.concatenate(acts, axis=1)
        yield
        ffn = _dot(act, w_down_ref[...])
        yield
        h = h + ffn
        hn = _rms(h, ple_norm_ref[...]).astype(_MXU_DTYPE)
        ple_gate = _dot(hn, ple_w_gate_ref[...])
        ple_proj = _dot(p_ref[rows, :].astype(_MXU_DTYPE), ple_w_proj_ref[...])
        yield
        h = h + jax.nn.sigmoid(ple_gate) * ple_proj
        if is_last:
            h = _rms(h, final_norm_ref[...])
        o_ref[rows, :] = h

    rb = h_ref.shape[0] // ROW_CHAINS
    blocks = [row_block(pl.ds(c * rb, rb)) for c in range(ROW_CHAINS)]
    while blocks:
        blocks = [b for b in blocks if next(b, StopIteration) is not StopIteration]

    if mixer == "conv":
        carry_ref[...] = carry["prev"]
    for src, dst in zip(next_f32_refs, next_bf16_refs):
        dst[...] = src[...].astype(_MXU_DTYPE)


def _const_spec(shape, index):
    return pl.BlockSpec(shape, lambda i: index, pipeline_mode=pl.Buffered(1))


def _stacked(arr, layer):
    tail = arr.shape[1:]
    return _const_spec((None,) + tail, (layer,) + (0,) * len(tail))


def _row_blocks(rows, n_steps):
    n = n_steps
    while rows % (n * BF16_ROWS):
        assert n % 2 == 0, (rows, n_steps)
        n //= 2
    return n


def _run_layer(h, p, layer, mixer, big, small, norms, final_norm, next_big, seq_len,
               scratch_shapes):
    t, d = h.shape
    tm = ROW_TILE
    assert t % tm == 0 and seq_len % tm == 0
    n_steps = t // tm
    is_last = final_norm is not None

    operands = [h, p]
    in_specs = [pl.BlockSpec((tm, d), lambda i: (i, 0)),
                pl.BlockSpec((None, tm, p.shape[-1]), lambda i: (layer, i, 0))]
    for arr in norms:
        operands.append(arr)
        in_specs.append(_stacked(arr, layer))
    own_layers = None
    if isinstance(big[0], tuple):
        own_layers = tuple(idx for _, idx in big)
        scratch_shapes = list(scratch_shapes)
        scratch_shapes += [pltpu.VMEM(arr.shape[1:], _MXU_DTYPE) for arr, _ in big]
        scratch_shapes += [pltpu.VMEM((2, STAGE_ROWS, STAGE_COLS), jnp.float32),
                           pltpu.SemaphoreType.DMA((2,))]
        for arr, _ in big:
            operands.append(arr)
            in_specs.append(pl.BlockSpec(memory_space=pl.ANY))
    else:
        for arr in big:
            operands.append(arr)
            in_specs.append(_const_spec(arr.shape, (0, 0)))
    for arr, idx in small:
        operands.append(arr)
        in_specs.append(_stacked(arr, idx))
    if is_last:
        operands.append(final_norm)
        in_specs.append(_const_spec(final_norm.shape, (0, 0)))

    out_shape = [jax.ShapeDtypeStruct((t, d), jnp.float32)]
    out_specs = [pl.BlockSpec((tm, d), lambda i: (i, 0))]
    for arr, idx in next_big:
        _, rows, cols = arr.shape
        n_blocks = _row_blocks(rows, n_steps)
        per = n_steps // n_blocks
        operands.append(arr)
        in_specs.append(pl.BlockSpec((None, rows // n_blocks, cols),
                                     lambda i, idx=idx, per=per: (idx, i // per, 0)))
        out_shape.append(jax.ShapeDtypeStruct((rows, cols), _MXU_DTYPE))
        out_specs.append(pl.BlockSpec((rows // n_blocks, cols),
                                      lambda i, per=per: (i // per, 0)))

    kernel = functools.partial(_layer_kernel, mixer=mixer, n_small=len(small), is_last=is_last,
                               n_next=len(next_big), tiles_per_seq=seq_len // tm,
                               own_layers=own_layers)
    outs = pl.pallas_call(
        kernel,
        grid=(n_steps,),
        in_specs=in_specs,
        out_specs=out_specs,
        out_shape=out_shape,
        scratch_shapes=scratch_shapes,
        compiler_params=pltpu.CompilerParams(
            dimension_semantics=("arbitrary",),
            vmem_limit_bytes=VMEM_LIMIT_BYTES),
        name=f"layer{layer}_{mixer}",
    )(*operands)
    return outs[0], list(outs[1:])


def kernel(x, p, mix_norm, conv_w_in, conv_w, conv_w_out, sg_w_in, sg_v_gain, sg_v_bias,
           sg_w_spatial, sg_b_spatial, sg_w_out, ffn_norm, ffn_w_gate, ffn_w_up, ffn_w_down,
           ple_norm, ple_w_gate, ple_w_proj, final_norm):
    bsz, seq_len, d = x.shape
    depth = p.shape[0]
    t = bsz * seq_len
    n_groups, chunk = sg_b_spatial.shape[1:]
    gd = d // n_groups
    assert (ROW_TILE // ROW_CHAINS) % chunk == 0

    vec = lambda a: a.reshape(a.shape[0], 1, a.shape[-1])
    norms = (vec(mix_norm), vec(ffn_norm), vec(ple_norm))
    b_s_tile = jnp.repeat(jnp.swapaxes(sg_b_spatial, 1, 2), gd, axis=2)
    sg_gain, sg_bias = vec(sg_v_gain), vec(sg_v_bias)

    def big_f32(layer):
        j = layer // 2
        mix = [(conv_w_in, j), (conv_w_out, j)] if layer % 2 == 0 else [(sg_w_in, j), (sg_w_out, j)]
        return mix + [(w, layer) for w in (ffn_w_gate, ffn_w_up, ffn_w_down, ple_w_gate, ple_w_proj)]

    h = x.reshape(t, d)
    p2 = p.reshape(depth, t, p.shape[-1])
    big = big_f32(0)
    for layer in range(depth):
        last = layer == depth - 1
        fn = final_norm.reshape(1, d) if last else None
        next_big = [] if last else big_f32(layer + 1)
        j = layer // 2
        if layer % 2 == 0:
            mixer, small = "conv", [(conv_w, j)]
            scratch = [pltpu.VMEM((SUBLANES, d), jnp.float32)]
        else:
            mixer = "sg"
            small = [(sg_gain, j), (sg_bias, j), (sg_w_spatial, j), (b_s_tile, j)]
            scratch = [pltpu.VMEM((ROW_TILE, d), _MXU_DTYPE),
                       pltpu.VMEM((ROW_TILE, d), _MXU_DTYPE)]
        h, big = _run_layer(h, p2, layer, mixer, big, small, norms, fn, next_big, seq_len,
                            scratch)
    return h.reshape(bsz, seq_len, d)
```

```python
import functools

import jax
import jax.numpy as jnp
from jax import lax
from jax.experimental import pallas as pl
from jax.experimental.pallas import tpu as pltpu

RMS_EPS = 1e-6
LN_EPS = 1e-5
SUBLANES = 8
BF16_ROWS = 16
ROW_TILE = 512
ROW_CHAINS = 2
FFN_COLS = 256
STAGE_ROWS, STAGE_COLS = 256, 1024
STAGE_SLOTS = 4
VMEM_LIMIT_BYTES = 58 * 1024 * 1024

_MXU_DTYPE = jnp.bfloat16
N_BIG = 7


def _rms(x, g):
    var = jnp.mean(x * x, axis=-1, keepdims=True)
    return (x * lax.rsqrt(var + RMS_EPS)) * g


def _dot(a, b):
    return jnp.dot(a, b, preferred_element_type=jnp.float32)


def _conv_gate(bcx, prev, w_conv_ref):
    tm = bcx.shape[0]
    d = bcx.shape[1] // 3
    b_gate = bcx[:, :d]
    z = bcx[:, d:2 * d] * bcx[:, 2 * d:]

    w = w_conv_ref[...]
    w0, w1, w2 = w[0:1, :], w[1:2, :], w[2:3, :]
    y = w2 * z + w1 * pltpu.roll(z, 1, axis=0) + w0 * pltpu.roll(z, 2, axis=0)
    row = lax.broadcasted_iota(jnp.int32, (SUBLANES, d), 0)
    zh = z[:SUBLANES, :]
    z1h = jnp.where(row < 1, pltpu.roll(prev, 1, axis=0), pltpu.roll(zh, 1, axis=0))
    z2h = jnp.where(row < 2, pltpu.roll(prev, 2, axis=0), pltpu.roll(zh, 2, axis=0))
    yh = w2 * zh + w1 * z1h + w0 * z2h
    y = jnp.concatenate([yh, y[SUBLANES:, :]], axis=0)
    return (b_gate * y).astype(_MXU_DTYPE), z[tm - SUBLANES:, :]


def _sg_gate(uv, gain_ref, bias_ref, w_s_ref, b_s_ref, v_ref, y_ref):
    tm = uv.shape[0]
    d = uv.shape[1] // 2
    n_groups, chunk, _ = w_s_ref.shape
    gd = d // n_groups
    n_chunks = tm // chunk
    u = uv[:, :d]
    v = uv[:, d:]
    mu = jnp.mean(v, axis=-1, keepdims=True)
    vc = v - mu
    var = jnp.mean(vc * vc, axis=-1, keepdims=True)
    v = (vc * lax.rsqrt(var + LN_EPS)) * gain_ref[...] + bias_ref[...]
    v_ref[...] = v.astype(_MXU_DTYPE)

    t_idx = lax.broadcasted_iota(jnp.int32, (chunk, chunk), 0)
    s_idx = lax.broadcasted_iota(jnp.int32, (chunk, chunk), 1)
    causal = s_idx <= t_idx
    b_s = b_s_ref[...]
    for g in range(n_groups):
        w_g = jnp.where(causal, w_s_ref[g], 0.0).astype(_MXU_DTYPE)
        cols = slice(g * gd, (g + 1) * gd)
        rhs = jnp.concatenate(
            [v_ref[c * chunk:(c + 1) * chunk, cols] for c in range(n_chunks)], axis=1)
        mixed = _dot(w_g, rhs)
        for c in range(n_chunks):
            rows = slice(c * chunk, (c + 1) * chunk)
            y_ref[rows, cols] = (
                u[rows, cols] * (mixed[:, c * gd:(c + 1) * gd] + b_s[:, cols])
            ).astype(_MXU_DTYPE)
    return y_ref[...]


def _round_weights_into_vmem(hbm_refs, layers, vmem_refs, stage_ref, sem_ref):
    n_slots = stage_ref.shape[0]
    chunks = []
    for w, dst in enumerate(vmem_refs):
        k, n = dst.shape
        assert k % STAGE_ROWS == 0
        chunks += [(w, r0, c0, min(STAGE_COLS, n - c0))
                   for r0 in range(0, k, STAGE_ROWS) for c0 in range(0, n, STAGE_COLS)]

    def copy(j):
        w, r0, c0, cols = chunks[j]
        return pltpu.make_async_copy(
            hbm_refs[w].at[layers[w], pl.ds(r0, STAGE_ROWS), pl.ds(c0, cols)],
            stage_ref.at[j % n_slots, :, pl.ds(0, cols)],
            sem_ref.at[j % n_slots])

    for j in range(min(n_slots - 1, len(chunks))):
        copy(j).start()
    for j, (w, r0, c0, cols) in enumerate(chunks):
        ahead = j + n_slots - 1
        if ahead < len(chunks):
            copy(ahead).start()
        copy(j).wait()
        vmem_refs[w][pl.ds(r0, STAGE_ROWS), pl.ds(c0, cols)] = (
            stage_ref[j % n_slots, :, pl.ds(0, cols)].astype(_MXU_DTYPE))


def _layer_kernel(*refs, mixer, n_small, is_last, n_next, tiles_per_seq, own_layers):
    refs = list(refs)
    take = lambda n: [refs.pop(0) for _ in range(n)]
    h_ref, p_ref, mix_norm_ref, ffn_norm_ref, ple_norm_ref = take(5)
    big_refs = take(N_BIG)
    small_refs = take(n_small)
    final_norm_ref = take(1)[0] if is_last else None
    next_f32_refs = take(n_next)
    o_ref = take(1)[0]
    next_bf16_refs = take(n_next)
    scratch = refs

    if own_layers is not None:
        w_vmem, (stage_ref, sem_ref) = scratch[-N_BIG - 2:-2], scratch[-2:]
        scratch = scratch[:-N_BIG - 2]

        @pl.when(pl.program_id(0) == 0)
        def _():
            _round_weights_into_vmem(big_refs, own_layers, w_vmem, stage_ref, sem_ref)

        big_refs = w_vmem
    (w_in_ref, w_out_ref, w_gate_ref, w_up_ref, w_down_ref,
     ple_w_gate_ref, ple_w_proj_ref) = big_refs

    carry = {}
    if mixer == "conv":
        carry_ref, = scratch

        @pl.when(pl.program_id(0) % tiles_per_seq == 0)
        def _():
            carry_ref[...] = jnp.zeros_like(carry_ref)

        carry["prev"] = carry_ref[...]

    def row_block(rows):
        h = h_ref[rows, :]
        hn = _rms(h, mix_norm_ref[...]).astype(_MXU_DTYPE)
        pre = _dot(hn, w_in_ref[...])
        yield
        if mixer == "conv":
            gated, carry["prev"] = _conv_gate(pre, carry["prev"], *small_refs)
        else:
            gated = _sg_gate(pre, *small_refs, *(s.at[rows, :] for s in scratch))
        mix = _dot(gated, w_out_ref[...])
        yield
        h = h + mix
        hn = _rms(h, ffn_norm_ref[...]).astype(_MXU_DTYPE)
        acts = []
        for n in range(w_gate_ref.shape[1] // FFN_COLS):
            cols = slice(n * FFN_COLS, (n + 1) * FFN_COLS)
            ffn_gate = _dot(hn, w_gate_ref[:, cols])
            ffn_up = _dot(hn, w_up_ref[:, cols])
            acts.append((jax.nn.silu(ffn_gate) * ffn_up).astype(_MXU_DTYPE))
        act = jnp.concatenate(acts, axis=1)
        yield
        ffn = _dot(act, w_down_ref[...])
        yield
        h = h + ffn
        hn = _rms(h, ple_norm_ref[...]).astype(_MXU_DTYPE)
        ple_gate = _dot(hn, ple_w_gate_ref[...])
        ple_proj = _dot(p_ref[rows, :].astype(_MXU_DTYPE), ple_w_proj_ref[...])
        yield
        h = h + jax.nn.sigmoid(ple_gate) * ple_proj
        if is_last:
            h = _rms(h, final_norm_ref[...])
        o_ref[rows, :] = h

    rb = h_ref.shape[0] // ROW_CHAINS
    blocks = [row_block(pl.ds(c * rb, rb)) for c in range(ROW_CHAINS)]
    while blocks:
        blocks = [b for b in blocks if next(b, StopIteration) is not StopIteration]

    if mixer == "conv":
        carry_ref[...] = carry["prev"]
    for src, dst in zip(next_f32_refs, next_bf16_refs):
        dst[...] = src[...].astype(_MXU_DTYPE)


def _const_spec(shape, index):
    return pl.BlockSpec(shape, lambda i: index, pipeline_mode=pl.Buffered(1))


def _stacked(arr, layer):
    tail = arr.shape[1:]
    return _const_spec((None,) + tail, (layer,) + (0,) * len(tail))


def _row_blocks(rows, n_steps):
    n = n_steps
    while rows % (n * BF16_ROWS):
        assert n % 2 == 0, (rows, n_steps)
        n //= 2
    return n


def _run_layer(h, p, layer, mixer, big, small, norms, final_norm, next_big, seq_len,
               scratch_shapes):
    t, d = h.shape
    tm = ROW_TILE
    assert t % tm == 0 and seq_len % tm == 0
    n_steps = t // tm
    is_last = final_norm is not None

    operands = [h, p]
    in_specs = [pl.BlockSpec((tm, d), lambda i: (i, 0)),
                pl.BlockSpec((None, tm, p.shape[-1]), lambda i: (layer, i, 0))]
    for arr in norms:
        operands.append(arr)
        in_specs.append(_stacked(arr, layer))
    own_layers = None
    if isinstance(big[0], tuple):
        own_layers = tuple(idx for _, idx in big)
        scratch_shapes = list(scratch_shapes)
        scratch_shapes += [pltpu.VMEM(arr.shape[1:], _MXU_DTYPE) for arr, _ in big]
        scratch_shapes += [pltpu.VMEM((STAGE_SLOTS, STAGE_ROWS, STAGE_COLS), jnp.float32),
                           pltpu.SemaphoreType.DMA((STAGE_SLOTS,))]
        for arr, _ in big:
            operands.append(arr)
            in_specs.append(pl.BlockSpec(memory_space=pl.ANY))
    else:
        for arr in big:
            operands.append(arr)
            in_specs.append(_const_spec(arr.shape, (0, 0)))
    for arr, idx in small:
        operands.append(arr)
        in_specs.append(_stacked(arr, idx))
    if is_last:
        operands.append(final_norm)
        in_specs.append(_const_spec(final_norm.shape, (0, 0)))

    out_shape = [jax.ShapeDtypeStruct((t, d), jnp.float32)]
    out_specs = [pl.BlockSpec((tm, d), lambda i: (i, 0))]
    for arr, idx in next_big:
        _, rows, cols = arr.shape
        n_blocks = _row_blocks(rows, n_steps)
        per = n_steps // n_blocks
        operands.append(arr)
        in_specs.append(pl.BlockSpec((None, rows // n_blocks, cols),
                                     lambda i, idx=idx, per=per: (idx, i // per, 0)))
        out_shape.append(jax.ShapeDtypeStruct((rows, cols), _MXU_DTYPE))
        out_specs.append(pl.BlockSpec((rows // n_blocks, cols),
                                      lambda i, per=per: (i // per, 0)))

    kernel = functools.partial(_layer_kernel, mixer=mixer, n_small=len(small), is_last=is_last,
                               n_next=len(next_big), tiles_per_seq=seq_len // tm,
                               own_layers=own_layers)
    outs = pl.pallas_call(
        kernel,
        grid=(n_steps,),
        in_specs=in_specs,
        out_specs=out_specs,
        out_shape=out_shape,
        scratch_shapes=scratch_shapes,
        compiler_params=pltpu.CompilerParams(
            dimension_semantics=("arbitrary",),
            vmem_limit_bytes=VMEM_LIMIT_BYTES),
        name=f"layer{layer}_{mixer}",
    )(*operands)
    return outs[0], list(outs[1:])


def kernel(x, p, mix_norm, conv_w_in, conv_w, conv_w_out, sg_w_in, sg_v_gain, sg_v_bias,
           sg_w_spatial, sg_b_spatial, sg_w_out, ffn_norm, ffn_w_gate, ffn_w_up, ffn_w_down,
           ple_norm, ple_w_gate, ple_w_proj, final_norm):
    bsz, seq_len, d = x.shape
    depth = p.shape[0]
    t = bsz * seq_len
    n_groups, chunk = sg_b_spatial.shape[1:]
    gd = d // n_groups
    assert (ROW_TILE // ROW_CHAINS) % chunk == 0

    vec = lambda a: a.reshape(a.shape[0], 1, a.shape[-1])
    norms = (vec(mix_norm), vec(ffn_norm), vec(ple_norm))
    b_s_tile = jnp.repeat(jnp.swapaxes(sg_b_spatial, 1, 2), gd, axis=2)
    sg_gain, sg_bias = vec(sg_v_gain), vec(sg_v_bias)

    def big_f32(layer):
        j = layer // 2
        mix = [(conv_w_in, j), (conv_w_out, j)] if layer % 2 == 0 else [(sg_w_in, j), (sg_w_out, j)]
        return mix + [(w, layer) for w in (ffn_w_gate, ffn_w_up, ffn_w_down, ple_w_gate, ple_w_proj)]

    h = x.reshape(t, d)
    p2 = p.reshape(depth, t, p.shape[-1])
    big = big_f32(0)
    for layer in range(depth):
        last = layer == depth - 1
        fn = final_norm.reshape(1, d) if last else None
        next_big = [] if last else big_f32(layer + 1)
        j = layer // 2
        if layer % 2 == 0:
            mixer, small = "conv", [(conv_w, j)]
            scratch = [pltpu.VMEM((SUBLANES, d), jnp.float32)]
        else:
            mixer = "sg"
            small = [(sg_gain, j), (sg_bias, j), (sg_w_spatial, j), (b_s_tile, j)]
            scratch = [pltpu.VMEM((ROW_TILE, d), _MXU_DTYPE),
                       pltpu.VMEM((ROW_TILE, d), _MXU_DTYPE)]
        h, big = _run_layer(h, p2, layer, mixer, big, small, norms, fn, next_big, seq_len,
                            scratch)
    return h.reshape(bsz, seq_len, d)
```

```python
import functools

import jax
import jax.numpy as jnp
from jax import lax
from jax.experimental import pallas as pl
from jax.experimental.pallas import tpu as pltpu

RMS_EPS = 1e-6
LN_EPS = 1e-5
SUBLANES = 8
BF16_ROWS = 16
ROW_TILE = 512
ROW_CHAINS = 2
FFN_COLS = 256
STAGE_ROWS, STAGE_COLS = 256, 1024
STAGE_SLOTS = 6
VMEM_LIMIT_BYTES = 58 * 1024 * 1024

_MXU_DTYPE = jnp.bfloat16
N_BIG = 7


def _rms(x, g):
    var = jnp.mean(x * x, axis=-1, keepdims=True)
    return (x * lax.rsqrt(var + RMS_EPS)) * g


def _dot(a, b):
    return jnp.dot(a, b, preferred_element_type=jnp.float32)


def _conv_gate(bcx, prev, w_conv_ref):
    tm = bcx.shape[0]
    d = bcx.shape[1] // 3
    b_gate = bcx[:, :d]
    z = bcx[:, d:2 * d] * bcx[:, 2 * d:]

    w = w_conv_ref[...]
    w0, w1, w2 = w[0:1, :], w[1:2, :], w[2:3, :]
    y = w2 * z + w1 * pltpu.roll(z, 1, axis=0) + w0 * pltpu.roll(z, 2, axis=0)
    row = lax.broadcasted_iota(jnp.int32, (SUBLANES, d), 0)
    zh = z[:SUBLANES, :]
    z1h = jnp.where(row < 1, pltpu.roll(prev, 1, axis=0), pltpu.roll(zh, 1, axis=0))
    z2h = jnp.where(row < 2, pltpu.roll(prev, 2, axis=0), pltpu.roll(zh, 2, axis=0))
    yh = w2 * zh + w1 * z1h + w0 * z2h
    y = jnp.concatenate([yh, y[SUBLANES:, :]], axis=0)
    return (b_gate * y).astype(_MXU_DTYPE), z[tm - SUBLANES:, :]


def _sg_gate(uv, gain_ref, bias_ref, w_s_ref, b_s_ref, v_ref, y_ref):
    tm = uv.shape[0]
    d = uv.shape[1] // 2
    n_groups, chunk, _ = w_s_ref.shape
    gd = d // n_groups
    n_chunks = tm // chunk
    u = uv[:, :d]
    v = uv[:, d:]
    mu = jnp.mean(v, axis=-1, keepdims=True)
    vc = v - mu
    var = jnp.mean(vc * vc, axis=-1, keepdims=True)
    v = (vc * lax.rsqrt(var + LN_EPS)) * gain_ref[...] + bias_ref[...]
    v_ref[...] = v.astype(_MXU_DTYPE)

    t_idx = lax.broadcasted_iota(jnp.int32, (chunk, chunk), 0)
    s_idx = lax.broadcasted_iota(jnp.int32, (chunk, chunk), 1)
    causal = s_idx <= t_idx
    b_s = b_s_ref[...]
    for g in range(n_groups):
        w_g = jnp.where(causal, w_s_ref[g], 0.0).astype(_MXU_DTYPE)
        cols = slice(g * gd, (g + 1) * gd)
        rhs = jnp.concatenate(
            [v_ref[c * chunk:(c + 1) * chunk, cols] for c in range(n_chunks)], axis=1)
        mixed = _dot(w_g, rhs)
        for c in range(n_chunks):
            rows = slice(c * chunk, (c + 1) * chunk)
            y_ref[rows, cols] = (
                u[rows, cols] * (mixed[:, c * gd:(c + 1) * gd] + b_s[:, cols])
            ).astype(_MXU_DTYPE)
    return y_ref[...]


def _round_weights_into_vmem(hbm_refs, layers, vmem_refs, stage_ref, sem_ref):
    n_slots = stage_ref.shape[0]
    chunks = []
    for w, dst in enumerate(vmem_refs):
        k, n = dst.shape
        assert k % STAGE_ROWS == 0
        chunks += [(w, r0, c0, min(STAGE_COLS, n - c0))
                   for r0 in range(0, k, STAGE_ROWS) for c0 in range(0, n, STAGE_COLS)]

    def copy(j):
        w, r0, c0, cols = chunks[j]
        return pltpu.make_async_copy(
            hbm_refs[w].at[layers[w], pl.ds(r0, STAGE_ROWS), pl.ds(c0, cols)],
            stage_ref.at[j % n_slots, :, pl.ds(0, cols)],
            sem_ref.at[j % n_slots])

    for j in range(min(n_slots - 1, len(chunks))):
        copy(j).start()
    for j, (w, r0, c0, cols) in enumerate(chunks):
        ahead = j + n_slots - 1
        if ahead < len(chunks):
            copy(ahead).start()
        copy(j).wait()
        vmem_refs[w][pl.ds(r0, STAGE_ROWS), pl.ds(c0, cols)] = (
            stage_ref[j % n_slots, :, pl.ds(0, cols)].astype(_MXU_DTYPE))


def _layer_kernel(*refs, mixer, layer, mixer_layer, n_small, n_small_rows, is_last, n_next,
                  tiles_per_seq, own_layers):
    refs = list(refs)
    take = lambda n: [refs.pop(0) for _ in range(n)]
    h_ref, p_ref = take(2)
    mix_norm_ref, ffn_norm_ref, ple_norm_ref = (r.at[pl.ds(layer, 1), :] for r in take(3))
    big_refs = take(N_BIG)
    small_refs = take(n_small)
    small_refs[:n_small_rows] = [r.at[pl.ds(mixer_layer, 1), :]
                                 for r in small_refs[:n_small_rows]]
    final_norm_ref = take(1)[0] if is_last else None
    next_f32_refs = take(n_next)
    o_ref = take(1)[0]
    next_bf16_refs = take(n_next)
    scratch = refs

    if own_layers is not None:
        w_vmem, (stage_ref, sem_ref) = scratch[-N_BIG - 2:-2], scratch[-2:]
        scratch = scratch[:-N_BIG - 2]

        @pl.when(pl.program_id(0) == 0)
        def _():
            _round_weights_into_vmem(big_refs, own_layers, w_vmem, stage_ref, sem_ref)

        big_refs = w_vmem
    (w_in_ref, w_out_ref, w_gate_ref, w_up_ref, w_down_ref,
     ple_w_gate_ref, ple_w_proj_ref) = big_refs

    carry = {}
    if mixer == "conv":
        carry_ref, = scratch

        @pl.when(pl.program_id(0) % tiles_per_seq == 0)
        def _():
            carry_ref[...] = jnp.zeros_like(carry_ref)

        carry["prev"] = carry_ref[...]

    def row_block(rows):
        h = h_ref[rows, :]
        hn = _rms(h, mix_norm_ref[...]).astype(_MXU_DTYPE)
        pre = _dot(hn, w_in_ref[...])
        yield
        if mixer == "conv":
            gated, carry["prev"] = _conv_gate(pre, carry["prev"], *small_refs)
        else:
            gated = _sg_gate(pre, *small_refs, *(s.at[rows, :] for s in scratch))
        mix = _dot(gated, w_out_ref[...])
        yield
        h = h + mix
        hn = _rms(h, ffn_norm_ref[...]).astype(_MXU_DTYPE)
        acts = []
        for n in range(w_gate_ref.shape[1] // FFN_COLS):
            cols = slice(n * FFN_COLS, (n + 1) * FFN_COLS)
            ffn_gate = _dot(hn, w_gate_ref[:, cols])
            ffn_up = _dot(hn, w_up_ref[:, cols])
            acts.append((jax.nn.silu(ffn_gate) * ffn_up).astype(_MXU_DTYPE))
        act = jnp.concatenate(acts, axis=1)
        yield
        ffn = _dot(act, w_down_ref[...])
        yield
        h = h + ffn
        hn = _rms(h, ple_norm_ref[...]).astype(_MXU_DTYPE)
        ple_gate = _dot(hn, ple_w_gate_ref[...])
        ple_proj = _dot(p_ref[rows, :].astype(_MXU_DTYPE), ple_w_proj_ref[...])
        yield
        h = h + jax.nn.sigmoid(ple_gate) * ple_proj
        if is_last:
            h = _rms(h, final_norm_ref[...])
        o_ref[rows, :] = h

    rb = h_ref.shape[0] // ROW_CHAINS
    blocks = [row_block(pl.ds(c * rb, rb)) for c in range(ROW_CHAINS)]
    while blocks:
        blocks = [b for b in blocks if next(b, StopIteration) is not StopIteration]

    if mixer == "conv":
        carry_ref[...] = carry["prev"]
    for src, dst in zip(next_f32_refs, next_bf16_refs):
        dst[...] = src[...].astype(_MXU_DTYPE)


def _const_spec(shape, index):
    return pl.BlockSpec(shape, lambda i: index, pipeline_mode=pl.Buffered(1))


def _stacked(arr, layer):
    tail = arr.shape[1:]
    return _const_spec((None,) + tail, (layer,) + (0,) * len(tail))


def _row_blocks(rows, n_steps):
    n = n_steps
    while rows % (n * BF16_ROWS):
        assert n % 2 == 0, (rows, n_steps)
        n //= 2
    return n


def _run_layer(h, p, layer, mixer, mixer_layer, big, small, norms, final_norm, next_big, seq_len,
               scratch_shapes):
    t, d = h.shape
    tm = ROW_TILE
    assert t % tm == 0 and seq_len % tm == 0
    n_steps = t // tm
    is_last = final_norm is not None

    operands = [h, p]
    in_specs = [pl.BlockSpec((tm, d), lambda i: (i, 0)),
                pl.BlockSpec((None, tm, p.shape[-1]), lambda i: (layer, i, 0))]
    for arr in norms:
        operands.append(arr)
        in_specs.append(_const_spec(arr.shape, (0, 0)))
    own_layers = None
    if isinstance(big[0], tuple):
        own_layers = tuple(idx for _, idx in big)
        scratch_shapes = list(scratch_shapes)
        scratch_shapes += [pltpu.VMEM(arr.shape[1:], _MXU_DTYPE) for arr, _ in big]
        scratch_shapes += [pltpu.VMEM((STAGE_SLOTS, STAGE_ROWS, STAGE_COLS), jnp.float32),
                           pltpu.SemaphoreType.DMA((STAGE_SLOTS,))]
        for arr, _ in big:
            operands.append(arr)
            in_specs.append(pl.BlockSpec(memory_space=pl.ANY))
    else:
        for arr in big:
            operands.append(arr)
            in_specs.append(_const_spec(arr.shape, (0, 0)))
    n_small_rows = sum(arr.ndim == 2 for arr in small)
    assert all(arr.ndim == 2 for arr in small[:n_small_rows])
    for arr in small:
        operands.append(arr)
        in_specs.append(_const_spec(arr.shape, (0, 0)) if arr.ndim == 2
                        else _stacked(arr, mixer_layer))
    if is_last:
        operands.append(final_norm)
        in_specs.append(_const_spec(final_norm.shape, (0, 0)))

    out_shape = [jax.ShapeDtypeStruct((t, d), jnp.float32)]
    out_specs = [pl.BlockSpec((tm, d), lambda i: (i, 0))]
    for arr, idx in next_big:
        _, rows, cols = arr.shape
        n_blocks = _row_blocks(rows, n_steps)
        per = n_steps // n_blocks
        operands.append(arr)
        in_specs.append(pl.BlockSpec((None, rows // n_blocks, cols),
                                     lambda i, idx=idx, per=per: (idx, i // per, 0)))
        out_shape.append(jax.ShapeDtypeStruct((rows, cols), _MXU_DTYPE))
        out_specs.append(pl.BlockSpec((rows // n_blocks, cols),
                                      lambda i, per=per: (i // per, 0)))

    kernel = functools.partial(_layer_kernel, mixer=mixer, layer=layer, mixer_layer=mixer_layer,
                               n_small=len(small), n_small_rows=n_small_rows, is_last=is_last,
                               n_next=len(next_big), tiles_per_seq=seq_len // tm,
                               own_layers=own_layers)
    outs = pl.pallas_call(
        kernel,
        grid=(n_steps,),
        in_specs=in_specs,
        out_specs=out_specs,
        out_shape=out_shape,
        scratch_shapes=scratch_shapes,
        compiler_params=pltpu.CompilerParams(
            dimension_semantics=("arbitrary",),
            vmem_limit_bytes=VMEM_LIMIT_BYTES),
        name=f"layer{layer}_{mixer}",
    )(*operands)
    return outs[0], list(outs[1:])


def kernel(x, p, mix_norm, conv_w_in, conv_w, conv_w_out, sg_w_in, sg_v_gain, sg_v_bias,
           sg_w_spatial, sg_b_spatial, sg_w_out, ffn_norm, ffn_w_gate, ffn_w_up, ffn_w_down,
           ple_norm, ple_w_gate, ple_w_proj, final_norm):
    bsz, seq_len, d = x.shape
    depth = p.shape[0]
    t = bsz * seq_len
    n_groups, chunk = sg_b_spatial.shape[1:]
    gd = d // n_groups
    assert (ROW_TILE // ROW_CHAINS) % chunk == 0

    norms = (mix_norm, ffn_norm, ple_norm)
    b_s_tile = jnp.repeat(jnp.swapaxes(sg_b_spatial, 1, 2), gd, axis=2)

    def big_f32(layer):
        j = layer // 2
        mix = [(conv_w_in, j), (conv_w_out, j)] if layer % 2 == 0 else [(sg_w_in, j), (sg_w_out, j)]
        return mix + [(w, layer) for w in (ffn_w_gate, ffn_w_up, ffn_w_down, ple_w_gate, ple_w_proj)]

    h = x.reshape(t, d)
    p2 = p.reshape(depth, t, p.shape[-1])
    big = big_f32(0)
    for layer in range(depth):
        last = layer == depth - 1
        fn = final_norm.reshape(1, d) if last else None
        next_big = [] if last else big_f32(layer + 1)
        j = layer // 2
        if layer % 2 == 0:
            mixer, small = "conv", [conv_w]
            scratch = [pltpu.VMEM((SUBLANES, d), jnp.float32)]
        else:
            mixer = "sg"
            small = [sg_v_gain, sg_v_bias, sg_w_spatial, b_s_tile]
            scratch = [pltpu.VMEM((ROW_TILE, d), _MXU_DTYPE),
                       pltpu.VMEM((ROW_TILE, d), _MXU_DTYPE)]
        h, big = _run_layer(h, p2, layer, mixer, j, big, small, norms, fn, next_big, seq_len,
                            scratch)
    return h.reshape(bsz, seq_len, d)
```

```python
import functools

import jax
import jax.numpy as jnp
from jax import lax
from jax.experimental import pallas as pl
from jax.experimental.pallas import tpu as pltpu

RMS_EPS = 1e-6
LN_EPS = 1e-5
SUBLANES = 8
BF16_ROWS = 16
ROW_TILE = 256
ROW_CHAINS = 1
FFN_COLS = 256
STAGE_ROWS, STAGE_COLS = 256, 1024
STAGE_SLOTS = 6
VMEM_LIMIT_BYTES = 58 * 1024 * 1024

_MXU_DTYPE = jnp.bfloat16
N_BIG = 7


def _rms(x, g):
    var = jnp.mean(x * x, axis=-1, keepdims=True)
    return (x * lax.rsqrt(var + RMS_EPS)) * g


def _dot(a, b):
    return jnp.dot(a, b, preferred_element_type=jnp.float32)


def _conv_gate(bcx, prev, w_conv_ref):
    tm = bcx.shape[0]
    d = bcx.shape[1] // 3
    b_gate = bcx[:, :d]
    z = bcx[:, d:2 * d] * bcx[:, 2 * d:]

    w = w_conv_ref[...]
    w0, w1, w2 = w[0:1, :], w[1:2, :], w[2:3, :]
    y = w2 * z + w1 * pltpu.roll(z, 1, axis=0) + w0 * pltpu.roll(z, 2, axis=0)
    row = lax.broadcasted_iota(jnp.int32, (SUBLANES, d), 0)
    zh = z[:SUBLANES, :]
    z1h = jnp.where(row < 1, pltpu.roll(prev, 1, axis=0), pltpu.roll(zh, 1, axis=0))
    z2h = jnp.where(row < 2, pltpu.roll(prev, 2, axis=0), pltpu.roll(zh, 2, axis=0))
    yh = w2 * zh + w1 * z1h + w0 * z2h
    y = jnp.concatenate([yh, y[SUBLANES:, :]], axis=0)
    return (b_gate * y).astype(_MXU_DTYPE), z[tm - SUBLANES:, :]


def _sg_gate(uv, gain_ref, bias_ref, w_s_ref, b_s_ref, v_ref, y_ref):
    tm = uv.shape[0]
    d = uv.shape[1] // 2
    n_groups, chunk, _ = w_s_ref.shape
    gd = d // n_groups
    n_chunks = tm // chunk
    u = uv[:, :d]
    v = uv[:, d:]
    mu = jnp.mean(v, axis=-1, keepdims=True)
    vc = v - mu
    var = jnp.mean(vc * vc, axis=-1, keepdims=True)
    v = (vc * lax.rsqrt(var + LN_EPS)) * gain_ref[...] + bias_ref[...]
    v_ref[...] = v.astype(_MXU_DTYPE)

    t_idx = lax.broadcasted_iota(jnp.int32, (chunk, chunk), 0)
    s_idx = lax.broadcasted_iota(jnp.int32, (chunk, chunk), 1)
    causal = s_idx <= t_idx
    b_s = b_s_ref[...]
    for g in range(n_groups):
        w_g = jnp.where(causal, w_s_ref[g], 0.0).astype(_MXU_DTYPE)
        cols = slice(g * gd, (g + 1) * gd)
        rhs = jnp.concatenate(
            [v_ref[c * chunk:(c + 1) * chunk, cols] for c in range(n_chunks)], axis=1)
        mixed = _dot(w_g, rhs)
        for c in range(n_chunks):
            rows = slice(c * chunk, (c + 1) * chunk)
            y_ref[rows, cols] = (
                u[rows, cols] * (mixed[:, c * gd:(c + 1) * gd] + b_s[:, cols])
            ).astype(_MXU_DTYPE)
    return y_ref[...]


def _round_weights_into_vmem(hbm_refs, layers, vmem_refs, stage_ref, sem_ref):
    n_slots = stage_ref.shape[0]
    chunks = []
    for w, dst in enumerate(vmem_refs):
        k, n = dst.shape
        assert k % STAGE_ROWS == 0
        chunks += [(w, r0, c0, min(STAGE_COLS, n - c0))
                   for r0 in range(0, k, STAGE_ROWS) for c0 in range(0, n, STAGE_COLS)]

    def copy(j):
        w, r0, c0, cols = chunks[j]
        return pltpu.make_async_copy(
            hbm_refs[w].at[layers[w], pl.ds(r0, STAGE_ROWS), pl.ds(c0, cols)],
            stage_ref.at[j % n_slots, :, pl.ds(0, cols)],
            sem_ref.at[j % n_slots])

    for j in range(min(n_slots - 1, len(chunks))):
        copy(j).start()
    for j, (w, r0, c0, cols) in enumerate(chunks):
        ahead = j + n_slots - 1
        if ahead < len(chunks):
            copy(ahead).start()
        copy(j).wait()
        vmem_refs[w][pl.ds(r0, STAGE_ROWS), pl.ds(c0, cols)] = (
            stage_ref[j % n_slots, :, pl.ds(0, cols)].astype(_MXU_DTYPE))


def _layer_kernel(*refs, mixer, layer, mixer_layer, n_small, n_small_rows, is_last, n_next,
                  tiles_per_seq, own_layers):
    refs = list(refs)
    take = lambda n: [refs.pop(0) for _ in range(n)]
    h_ref, p_ref = take(2)
    mix_norm_ref, ffn_norm_ref, ple_norm_ref = (r.at[pl.ds(layer, 1), :] for r in take(3))
    big_refs = take(N_BIG)
    small_refs = take(n_small)
    small_refs[:n_small_rows] = [r.at[pl.ds(mixer_layer, 1), :]
                                 for r in small_refs[:n_small_rows]]
    final_norm_ref = take(1)[0] if is_last else None
    next_f32_refs = take(n_next)
    o_ref = take(1)[0]
    next_bf16_refs = take(n_next)
    scratch = refs

    if own_layers is not None:
        w_vmem, (stage_ref, sem_ref) = scratch[-N_BIG - 2:-2], scratch[-2:]
        scratch = scratch[:-N_BIG - 2]

        @pl.when(pl.program_id(0) == 0)
        def _():
            _round_weights_into_vmem(big_refs, own_layers, w_vmem, stage_ref, sem_ref)

        big_refs = w_vmem
    (w_in_ref, w_out_ref, w_gate_ref, w_up_ref, w_down_ref,
     ple_w_gate_ref, ple_w_proj_ref) = big_refs

    carry = {}
    if mixer == "conv":
        carry_ref, = scratch

        @pl.when(pl.program_id(0) % tiles_per_seq == 0)
        def _():
            carry_ref[...] = jnp.zeros_like(carry_ref)

        carry["prev"] = carry_ref[...]

    def row_block(rows):
        h = h_ref[rows, :]
        hn = _rms(h, mix_norm_ref[...]).astype(_MXU_DTYPE)
        pre = _dot(hn, w_in_ref[...])
        yield
        if mixer == "conv":
            gated, carry["prev"] = _conv_gate(pre, carry["prev"], *small_refs)
        else:
            gated = _sg_gate(pre, *small_refs, *(s.at[rows, :] for s in scratch))
        mix = _dot(gated, w_out_ref[...])
        yield
        h = h + mix
        hn = _rms(h, ffn_norm_ref[...]).astype(_MXU_DTYPE)
        acts = []
        for n in range(w_gate_ref.shape[1] // FFN_COLS):
            cols = slice(n * FFN_COLS, (n + 1) * FFN_COLS)
            ffn_gate = _dot(hn, w_gate_ref[:, cols])
            ffn_up = _dot(hn, w_up_ref[:, cols])
            acts.append((jax.nn.silu(ffn_gate) * ffn_up).astype(_MXU_DTYPE))
        act = jnp.concatenate(acts, axis=1)
        yield
        ffn = _dot(act, w_down_ref[...])
        yield
        h = h + ffn
        hn = _rms(h, ple_norm_ref[...]).astype(_MXU_DTYPE)
        ple_gate = _dot(hn, ple_w_gate_ref[...])
        ple_proj = _dot(p_ref[rows, :].astype(_MXU_DTYPE), ple_w_proj_ref[...])
        yield
        h = h + jax.nn.sigmoid(ple_gate) * ple_proj
        if is_last:
            h = _rms(h, final_norm_ref[...])
        o_ref[rows, :] = h

    rb = h_ref.shape[0] // ROW_CHAINS
    blocks = [row_block(pl.ds(c * rb, rb)) for c in range(ROW_CHAINS)]
    while blocks:
        blocks = [b for b in blocks if next(b, StopIteration) is not StopIteration]

    if mixer == "conv":
        carry_ref[...] = carry["prev"]
    for src, dst in zip(next_f32_refs, next_bf16_refs):
        dst[...] = src[...].astype(_MXU_DTYPE)


def _const_spec(shape, index):
    return pl.BlockSpec(shape, lambda i: index, pipeline_mode=pl.Buffered(1))


def _stacked(arr, layer):
    tail = arr.shape[1:]
    return _const_spec((None,) + tail, (layer,) + (0,) * len(tail))


def _row_blocks(rows, n_steps):
    n = n_steps
    while rows % (n * BF16_ROWS):
        assert n % 2 == 0, (rows, n_steps)
        n //= 2
    return n


def _run_layer(h, p, layer, mixer, mixer_layer, big, small, norms, final_norm, next_big, seq_len,
               scratch_shapes):
    t, d = h.shape
    tm = ROW_TILE
    assert t % tm == 0 and seq_len % tm == 0
    n_steps = t // tm
    is_last = final_norm is not None

    operands = [h, p]
    in_specs = [pl.BlockSpec((tm, d), lambda i: (i, 0)),
                pl.BlockSpec((None, tm, p.shape[-1]), lambda i: (layer, i, 0))]
    for arr in norms:
        operands.append(arr)
        in_specs.append(_const_spec(arr.shape, (0, 0)))
    own_layers = None
    if isinstance(big[0], tuple):
        own_layers = tuple(idx for _, idx in big)
        scratch_shapes = list(scratch_shapes)
        scratch_shapes += [pltpu.VMEM(arr.shape[1:], _MXU_DTYPE) for arr, _ in big]
        scratch_shapes += [pltpu.VMEM((STAGE_SLOTS, STAGE_ROWS, STAGE_COLS), jnp.float32),
                           pltpu.SemaphoreType.DMA((STAGE_SLOTS,))]
        for arr, _ in big:
            operands.append(arr)
            in_specs.append(pl.BlockSpec(memory_space=pl.ANY))
    else:
        for arr in big:
            operands.append(arr)
            in_specs.append(_const_spec(arr.shape, (0, 0)))
    n_small_rows = sum(arr.ndim == 2 for arr in small)
    assert all(arr.ndim == 2 for arr in small[:n_small_rows])
    for arr in small:
        operands.append(arr)
        in_specs.append(_const_spec(arr.shape, (0, 0)) if arr.ndim == 2
                        else _stacked(arr, mixer_layer))
    if is_last:
        operands.append(final_norm)
        in_specs.append(_const_spec(final_norm.shape, (0, 0)))

    out_shape = [jax.ShapeDtypeStruct((t, d), jnp.float32)]
    out_specs = [pl.BlockSpec((tm, d), lambda i: (i, 0))]
    for arr, idx in next_big:
        _, rows, cols = arr.shape
        n_blocks = _row_blocks(rows, n_steps)
        per = n_steps // n_blocks
        operands.append(arr)
        in_specs.append(pl.BlockSpec((None, rows // n_blocks, cols),
                                     lambda i, idx=idx, per=per: (idx, i // per, 0)))
        out_shape.append(jax.ShapeDtypeStruct((rows, cols), _MXU_DTYPE))
        out_specs.append(pl.BlockSpec((rows // n_blocks, cols),
                                      lambda i, per=per: (i // per, 0)))

    kernel = functools.partial(_layer_kernel, mixer=mixer, layer=layer, mixer_layer=mixer_layer,
                               n_small=len(small), n_small_rows=n_small_rows, is_last=is_last,
                               n_next=len(next_big), tiles_per_seq=seq_len // tm,
                               own_layers=own_layers)
    outs = pl.pallas_call(
        kernel,
        grid=(n_steps,),
        in_specs=in_specs,
        out_specs=out_specs,
        out_shape=out_shape,
        scratch_shapes=scratch_shapes,
        compiler_params=pltpu.CompilerParams(
            dimension_semantics=("arbitrary",),
            vmem_limit_bytes=VMEM_LIMIT_BYTES),
        name=f"layer{layer}_{mixer}",
    )(*operands)
    return outs[0], list(outs[1:])


def kernel(x, p, mix_norm, conv_w_in, conv_w, conv_w_out, sg_w_in, sg_v_gain, sg_v_bias,
           sg_w_spatial, sg_b_spatial, sg_w_out, ffn_norm, ffn_w_gate, ffn_w_up, ffn_w_down,
           ple_norm, ple_w_gate, ple_w_proj, final_norm):
    bsz, seq_len, d = x.shape
    depth = p.shape[0]
    t = bsz * seq_len
    n_groups, chunk = sg_b_spatial.shape[1:]
    gd = d // n_groups
    assert (ROW_TILE // ROW_CHAINS) % chunk == 0

    norms = (mix_norm, ffn_norm, ple_norm)
    b_s_tile = jnp.repeat(jnp.swapaxes(sg_b_spatial, 1, 2), gd, axis=2)

    def big_f32(layer):
        j = layer // 2
        mix = [(conv_w_in, j), (conv_w_out, j)] if layer % 2 == 0 else [(sg_w_in, j), (sg_w_out, j)]
        return mix + [(w, layer) for w in (ffn_w_gate, ffn_w_up, ffn_w_down, ple_w_gate, ple_w_proj)]

    h = x.reshape(t, d)
    p2 = p.reshape(depth, t, p.shape[-1])
    big = big_f32(0)
    for layer in range(depth):
        last = layer == depth - 1
        fn = final_norm.reshape(1, d) if last else None
        next_big = [] if last else big_f32(layer + 1)
        j = layer // 2
        if layer % 2 == 0:
            mixer, small = "conv", [conv_w]
            scratch = [pltpu.VMEM((SUBLANES, d), jnp.float32)]
        else:
            mixer = "sg"
            small = [sg_v_gain, sg_v_bias, sg_w_spatial, b_s_tile]
            scratch = [pltpu.VMEM((ROW_TILE, d), _MXU_DTYPE),
                       pltpu.VMEM((ROW_TILE, d), _MXU_DTYPE)]
        h, big = _run_layer(h, p2, layer, mixer, j, big, small, norms, fn, next_big, seq_len,
                            scratch)
    return h.reshape(bsz, seq_len, d)
```

```python
import functools

import jax
import jax.numpy as jnp
from jax import lax
from jax.experimental import pallas as pl
from jax.experimental.pallas import tpu as pltpu

RMS_EPS = 1e-6
LN_EPS = 1e-5
SUBLANES = 8
BF16_ROWS = 16
ROW_TILE = 512
ROW_CHAINS = 2
MXU_COLS = 256
STAGE_ROWS, STAGE_COLS = 256, 1024
STAGE_SLOTS = 6
VMEM_LIMIT_BYTES = 58 * 1024 * 1024

_MXU_DTYPE = jnp.bfloat16
N_BIG = 7


def _rms(x, g):
    var = jnp.mean(x * x, axis=-1, keepdims=True)
    return (x * lax.rsqrt(var + RMS_EPS)) * g


def _dot(a, b):
    return jnp.dot(a, b, preferred_element_type=jnp.float32)


def _conv_gate(b_gate, z, prev, w_conv_ref):
    tm, d = z.shape

    w = w_conv_ref[...]
    w0, w1, w2 = w[0:1, :], w[1:2, :], w[2:3, :]
    y = w2 * z + w1 * pltpu.roll(z, 1, axis=0) + w0 * pltpu.roll(z, 2, axis=0)
    row = lax.broadcasted_iota(jnp.int32, (SUBLANES, d), 0)
    zh = z[:SUBLANES, :]
    z1h = jnp.where(row < 1, pltpu.roll(prev, 1, axis=0), pltpu.roll(zh, 1, axis=0))
    z2h = jnp.where(row < 2, pltpu.roll(prev, 2, axis=0), pltpu.roll(zh, 2, axis=0))
    yh = w2 * zh + w1 * z1h + w0 * z2h
    y = jnp.concatenate([yh, y[SUBLANES:, :]], axis=0)
    return (b_gate * y).astype(_MXU_DTYPE), z[tm - SUBLANES:, :]


def _sg_gate(u, v, gain_ref, bias_ref, w_s_ref, b_s_ref, v_ref, y_ref):
    tm, d = u.shape
    n_groups, chunk, _ = w_s_ref.shape
    gd = d // n_groups
    n_chunks = tm // chunk
    mu = jnp.mean(v, axis=-1, keepdims=True)
    vc = v - mu
    var = jnp.mean(vc * vc, axis=-1, keepdims=True)
    v = (vc * lax.rsqrt(var + LN_EPS)) * gain_ref[...] + bias_ref[...]
    v_ref[...] = v.astype(_MXU_DTYPE)

    t_idx = lax.broadcasted_iota(jnp.int32, (chunk, chunk), 0)
    s_idx = lax.broadcasted_iota(jnp.int32, (chunk, chunk), 1)
    causal = s_idx <= t_idx
    b_s = b_s_ref[...]
    for g in range(n_groups):
        w_g = jnp.where(causal, w_s_ref[g], 0.0).astype(_MXU_DTYPE)
        cols = slice(g * gd, (g + 1) * gd)
        rhs = jnp.concatenate(
            [v_ref[c * chunk:(c + 1) * chunk, cols] for c in range(n_chunks)], axis=1)
        mixed = _dot(w_g, rhs)
        for c in range(n_chunks):
            rows = slice(c * chunk, (c + 1) * chunk)
            y_ref[rows, cols] = (
                u[rows, cols] * (mixed[:, c * gd:(c + 1) * gd] + b_s[:, cols])
            ).astype(_MXU_DTYPE)
    return y_ref[...]


def _round_weights_into_vmem(hbm_refs, layers, vmem_refs, stage_ref, sem_ref):
    n_slots = stage_ref.shape[0]
    chunks = []
    for w, dst in enumerate(vmem_refs):
        k, n = dst.shape
        assert k % STAGE_ROWS == 0
        chunks += [(w, r0, c0, min(STAGE_COLS, n - c0))
                   for r0 in range(0, k, STAGE_ROWS) for c0 in range(0, n, STAGE_COLS)]

    def copy(j):
        w, r0, c0, cols = chunks[j]
        return pltpu.make_async_copy(
            hbm_refs[w].at[layers[w], pl.ds(r0, STAGE_ROWS), pl.ds(c0, cols)],
            stage_ref.at[j % n_slots, :, pl.ds(0, cols)],
            sem_ref.at[j % n_slots])

    for j in range(min(n_slots - 1, len(chunks))):
        copy(j).start()
    for j, (w, r0, c0, cols) in enumerate(chunks):
        ahead = j + n_slots - 1
        if ahead < len(chunks):
            copy(ahead).start()
        copy(j).wait()
        vmem_refs[w][pl.ds(r0, STAGE_ROWS), pl.ds(c0, cols)] = (
            stage_ref[j % n_slots, :, pl.ds(0, cols)].astype(_MXU_DTYPE))


def _layer_kernel(*refs, mixer, layer, mixer_layer, n_small, n_small_rows, is_last, n_next,
                  tiles_per_seq, own_layers):
    refs = list(refs)
    take = lambda n: [refs.pop(0) for _ in range(n)]
    h_ref, p_ref = take(2)
    mix_norm_ref, ffn_norm_ref, ple_norm_ref = (r.at[pl.ds(layer, 1), :] for r in take(3))
    big_refs = take(N_BIG)
    small_refs = take(n_small)
    small_refs[:n_small_rows] = [r.at[pl.ds(mixer_layer, 1), :]
                                 for r in small_refs[:n_small_rows]]
    final_norm_ref = take(1)[0] if is_last else None
    next_f32_refs = take(n_next)
    o_ref = take(1)[0]
    next_bf16_refs = take(n_next)
    scratch = refs

    if own_layers is not None:
        w_vmem, (stage_ref, sem_ref) = scratch[-N_BIG - 2:-2], scratch[-2:]
        scratch = scratch[:-N_BIG - 2]

        @pl.when(pl.program_id(0) == 0)
        def _():
            _round_weights_into_vmem(big_refs, own_layers, w_vmem, stage_ref, sem_ref)

        big_refs = w_vmem
    (w_in_ref, w_out_ref, w_gate_ref, w_up_ref, w_down_ref,
     ple_w_gate_ref, ple_w_proj_ref) = big_refs

    carry = {}
    if mixer == "conv":
        carry_ref, = scratch

        @pl.when(pl.program_id(0) % tiles_per_seq == 0)
        def _():
            carry_ref[...] = jnp.zeros_like(carry_ref)

        carry["prev"] = carry_ref[...]

    def row_block(rows):
        h = h_ref[rows, :]
        hn = _rms(h, mix_norm_ref[...]).astype(_MXU_DTYPE)
        d = h.shape[1]
        if mixer == "conv":
            b_gate = _dot(hn, w_in_ref[:, :d])
            z = jnp.concatenate(
                [_dot(hn, w_in_ref[:, d + n:d + n + MXU_COLS])
                 * _dot(hn, w_in_ref[:, 2 * d + n:2 * d + n + MXU_COLS])
                 for n in range(0, d, MXU_COLS)], axis=1)
        else:
            v = _dot(hn, w_in_ref[:, d:])
            u = _dot(hn, w_in_ref[:, :d])
        yield
        if mixer == "conv":
            gated, carry["prev"] = _conv_gate(b_gate, z, carry["prev"], *small_refs)
        else:
            gated = _sg_gate(u, v, *small_refs, *(s.at[rows, :] for s in scratch))
        mix = _dot(gated, w_out_ref[...])
        yield
        h = h + mix
        hn = _rms(h, ffn_norm_ref[...]).astype(_MXU_DTYPE)
        acts = []
        for n in range(w_gate_ref.shape[1] // MXU_COLS):
            cols = slice(n * MXU_COLS, (n + 1) * MXU_COLS)
            ffn_gate = _dot(hn, w_gate_ref[:, cols])
            ffn_up = _dot(hn, w_up_ref[:, cols])
            acts.append((jax.nn.silu(ffn_gate) * ffn_up).astype(_MXU_DTYPE))
        act = jnp.concatenate(acts, axis=1)
        yield
        ffn = _dot(act, w_down_ref[...])
        yield
        h = h + ffn
        hn = _rms(h, ple_norm_ref[...]).astype(_MXU_DTYPE)
        pb = p_ref[rows, :].astype(_MXU_DTYPE)
        h = jnp.concatenate(
            [h[:, n:n + MXU_COLS]
             + jax.nn.sigmoid(_dot(hn, ple_w_gate_ref[:, n:n + MXU_COLS]))
             * _dot(pb, ple_w_proj_ref[:, n:n + MXU_COLS])
             for n in range(0, d, MXU_COLS)], axis=1)
        yield
        if is_last:
            h = _rms(h, final_norm_ref[...])
        o_ref[rows, :] = h

    rb = h_ref.shape[0] // ROW_CHAINS
    blocks = [row_block(pl.ds(c * rb, rb)) for c in range(ROW_CHAINS)]
    while blocks:
        blocks = [b for b in blocks if next(b, StopIteration) is not StopIteration]

    if mixer == "conv":
        carry_ref[...] = carry["prev"]
    for src, dst in zip(next_f32_refs, next_bf16_refs):
        dst[...] = src[...].astype(_MXU_DTYPE)


def _const_spec(shape, index):
    return pl.BlockSpec(shape, lambda i: index, pipeline_mode=pl.Buffered(1))


def _stacked(arr, layer):
    tail = arr.shape[1:]
    return _const_spec((None,) + tail, (layer,) + (0,) * len(tail))


def _row_blocks(rows, n_steps):
    n = n_steps
    while rows % (n * BF16_ROWS):
        assert n % 2 == 0, (rows, n_steps)
        n //= 2
    return n


def _run_layer(h, p, layer, mixer, mixer_layer, big, small, norms, final_norm, next_big, seq_len,
               scratch_shapes):
    t, d = h.shape
    tm = ROW_TILE
    assert t % tm == 0 and seq_len % tm == 0
    n_steps = t // tm
    is_last = final_norm is not None

    operands = [h, p]
    in_specs = [pl.BlockSpec((tm, d), lambda i: (i, 0)),
                pl.BlockSpec((None, tm, p.shape[-1]), lambda i: (layer, i, 0))]
    for arr in norms:
        operands.append(arr)
        in_specs.append(_const_spec(arr.shape, (0, 0)))
    own_layers = None
    if isinstance(big[0], tuple):
        own_layers = tuple(idx for _, idx in big)
        scratch_shapes = list(scratch_shapes)
        scratch_shapes += [pltpu.VMEM(arr.shape[1:], _MXU_DTYPE) for arr, _ in big]
        scratch_shapes += [pltpu.VMEM((STAGE_SLOTS, STAGE_ROWS, STAGE_COLS), jnp.float32),
                           pltpu.SemaphoreType.DMA((STAGE_SLOTS,))]
        for arr, _ in big:
            operands.append(arr)
            in_specs.append(pl.BlockSpec(memory_space=pl.ANY))
    else:
        for arr in big:
            operands.append(arr)
            in_specs.append(_const_spec(arr.shape, (0, 0)))
    n_small_rows = sum(arr.ndim == 2 for arr in small)
    assert all(arr.ndim == 2 for arr in small[:n_small_rows])
    for arr in small:
        operands.append(arr)
        in_specs.append(_const_spec(arr.shape, (0, 0)) if arr.ndim == 2
                        else _stacked(arr, mixer_layer))
    if is_last:
        operands.append(final_norm)
        in_specs.append(_const_spec(final_norm.shape, (0, 0)))

    out_shape = [jax.ShapeDtypeStruct((t, d), jnp.float32)]
    out_specs = [pl.BlockSpec((tm, d), lambda i: (i, 0))]
    for arr, idx in next_big:
        _, rows, cols = arr.shape
        n_blocks = _row_blocks(rows, n_steps)
        per = n_steps // n_blocks
        operands.append(arr)
        in_specs.append(pl.BlockSpec((None, rows // n_blocks, cols),
                                     lambda i, idx=idx, per=per: (idx, i // per, 0)))
        out_shape.append(jax.ShapeDtypeStruct((rows, cols), _MXU_DTYPE))
        out_specs.append(pl.BlockSpec((rows // n_blocks, cols),
                                      lambda i, per=per: (i // per, 0)))

    kernel = functools.partial(_layer_kernel, mixer=mixer, layer=layer, mixer_layer=mixer_layer,
                               n_small=len(small), n_small_rows=n_small_rows, is_last=is_last,
                               n_next=len(next_big), tiles_per_seq=seq_len // tm,
                               own_layers=own_layers)
    outs = pl.pallas_call(
        kernel,
        grid=(n_steps,),
        in_specs=in_specs,
        out_specs=out_specs,
        out_shape=out_shape,
        scratch_shapes=scratch_shapes,
        compiler_params=pltpu.CompilerParams(
            dimension_semantics=("arbitrary",),
            vmem_limit_bytes=VMEM_LIMIT_BYTES),
        name=f"layer{layer}_{mixer}",
    )(*operands)
    return outs[0], list(outs[1:])


def kernel(x, p, mix_norm, conv_w_in, conv_w, conv_w_out, sg_w_in, sg_v_gain, sg_v_bias,
           sg_w_spatial, sg_b_spatial, sg_w_out, ffn_norm, ffn_w_gate, ffn_w_up, ffn_w_down,
           ple_norm, ple_w_gate, ple_w_proj, final_norm):
    bsz, seq_len, d = x.shape
    depth = p.shape[0]
    t = bsz * seq_len
    n_groups, chunk = sg_b_spatial.shape[1:]
    gd = d // n_groups
    assert (ROW_TILE // ROW_CHAINS) % chunk == 0

    norms = (mix_norm, ffn_norm, ple_norm)
    b_s_tile = jnp.repeat(jnp.swapaxes(sg_b_spatial, 1, 2), gd, axis=2)

    def big_f32(layer):
        j = layer // 2
        mix = [(conv_w_in, j), (conv_w_out, j)] if layer % 2 == 0 else [(sg_w_in, j), (sg_w_out, j)]
        return mix + [(w, layer) for w in (ffn_w_gate, ffn_w_up, ffn_w_down, ple_w_gate, ple_w_proj)]

    h = x.reshape(t, d)
    p2 = p.reshape(depth, t, p.shape[-1])
    big = big_f32(0)
    for layer in range(depth):
        last = layer == depth - 1
        fn = final_norm.reshape(1, d) if last else None
        next_big = [] if last else big_f32(layer + 1)
        j = layer // 2
        if layer % 2 == 0:
            mixer, small = "conv", [conv_w]
            scratch = [pltpu.VMEM((SUBLANES, d), jnp.float32)]
        else:
            mixer = "sg"
            small = [sg_v_gain, sg_v_bias, sg_w_spatial, b_s_tile]
            scratch = [pltpu.VMEM((ROW_TILE, d), _MXU_DTYPE),
                       pltpu.VMEM((ROW_TILE, d), _MXU_DTYPE)]
        h, big = _run_layer(h, p2, layer, mixer, j, big, small, norms, fn, next_big, seq_len,
                            scratch)
    return h.reshape(bsz, seq_len, d)
```

```python
import functools

import jax
import jax.numpy as jnp
from jax import lax
from jax.experimental import pallas as pl
from jax.experimental.pallas import tpu as pltpu

RMS_EPS = 1e-6
LN_EPS = 1e-5
SUBLANES = 8
BF16_ROWS = 16
ROW_TILE = 512
ROW_CHAINS = 2
MXU_COLS = 256
STAGE_ROWS, STAGE_COLS = 256, 1024
STAGE_SLOTS = 6
VMEM_LIMIT_BYTES = 58 * 1024 * 1024

_MXU_DTYPE = jnp.bfloat16
N_BIG = 7


def _rms(x, g):
    var = jnp.mean(x * x, axis=-1, keepdims=True)
    return (x * lax.rsqrt(var + RMS_EPS)) * g


def _dot(a, b):
    return jnp.dot(a, b, preferred_element_type=jnp.float32)


def _conv_gate(b_gate, z, prev, w_conv_ref):
    tm, d = z.shape

    w = w_conv_ref[...]
    w0, w1, w2 = w[0:1, :], w[1:2, :], w[2:3, :]
    y = w2 * z + w1 * pltpu.roll(z, 1, axis=0) + w0 * pltpu.roll(z, 2, axis=0)
    row = lax.broadcasted_iota(jnp.int32, (SUBLANES, d), 0)
    zh = z[:SUBLANES, :]
    z1h = jnp.where(row < 1, pltpu.roll(prev, 1, axis=0), pltpu.roll(zh, 1, axis=0))
    z2h = jnp.where(row < 2, pltpu.roll(prev, 2, axis=0), pltpu.roll(zh, 2, axis=0))
    yh = w2 * zh + w1 * z1h + w0 * z2h
    y = jnp.concatenate([yh, y[SUBLANES:, :]], axis=0)
    return (b_gate * y).astype(_MXU_DTYPE), z[tm - SUBLANES:, :]


def _sg_gate(hn, w_in_ref, v, gain_ref, bias_ref, w_s_ref, b_s_ref, v_ref, y_ref):
    tm, d = v.shape
    n_groups, chunk, _ = w_s_ref.shape
    gd = d // n_groups
    n_chunks = tm // chunk
    mu = jnp.mean(v, axis=-1, keepdims=True)
    vc = v - mu
    var = jnp.mean(vc * vc, axis=-1, keepdims=True)
    v = (vc * lax.rsqrt(var + LN_EPS)) * gain_ref[...] + bias_ref[...]
    v_ref[...] = v.astype(_MXU_DTYPE)

    t_idx = lax.broadcasted_iota(jnp.int32, (chunk, chunk), 0)
    s_idx = lax.broadcasted_iota(jnp.int32, (chunk, chunk), 1)
    causal = s_idx <= t_idx
    b_s = b_s_ref[...]
    for g in range(n_groups):
        if g * gd % MXU_COLS == 0:
            u_cols = g * gd
            u = _dot(hn, w_in_ref[:, u_cols:u_cols + MXU_COLS])
        w_g = jnp.where(causal, w_s_ref[g], 0.0).astype(_MXU_DTYPE)
        cols = slice(g * gd, (g + 1) * gd)
        ucols = slice(g * gd - u_cols, (g + 1) * gd - u_cols)
        rhs = jnp.concatenate(
            [v_ref[c * chunk:(c + 1) * chunk, cols] for c in range(n_chunks)], axis=1)
        mixed = _dot(w_g, rhs)
        for c in range(n_chunks):
            rows = slice(c * chunk, (c + 1) * chunk)
            y_ref[rows, cols] = (
                u[rows, ucols] * (mixed[:, c * gd:(c + 1) * gd] + b_s[:, cols])
            ).astype(_MXU_DTYPE)
    return y_ref[...]


def _round_weights_into_vmem(hbm_refs, layers, vmem_refs, stage_ref, sem_ref):
    n_slots = stage_ref.shape[0]
    chunks = []
    for w, dst in enumerate(vmem_refs):
        k, n = dst.shape
        assert k % STAGE_ROWS == 0
        chunks += [(w, r0, c0, min(STAGE_COLS, n - c0))
                   for r0 in range(0, k, STAGE_ROWS) for c0 in range(0, n, STAGE_COLS)]

    def copy(j):
        w, r0, c0, cols = chunks[j]
        return pltpu.make_async_copy(
            hbm_refs[w].at[layers[w], pl.ds(r0, STAGE_ROWS), pl.ds(c0, cols)],
            stage_ref.at[j % n_slots, :, pl.ds(0, cols)],
            sem_ref.at[j % n_slots])

    for j in range(min(n_slots - 1, len(chunks))):
        copy(j).start()
    for j, (w, r0, c0, cols) in enumerate(chunks):
        ahead = j + n_slots - 1
        if ahead < len(chunks):
            copy(ahead).start()
        copy(j).wait()
        vmem_refs[w][pl.ds(r0, STAGE_ROWS), pl.ds(c0, cols)] = (
            stage_ref[j % n_slots, :, pl.ds(0, cols)].astype(_MXU_DTYPE))


def _layer_kernel(*refs, mixer, layer, mixer_layer, n_small, n_small_rows, is_last, n_next,
                  tiles_per_seq, own_layers):
    refs = list(refs)
    take = lambda n: [refs.pop(0) for _ in range(n)]
    h_ref, p_ref = take(2)
    mix_norm_ref, ffn_norm_ref, ple_norm_ref = (r.at[pl.ds(layer, 1), :] for r in take(3))
    big_refs = take(N_BIG)
    small_refs = take(n_small)
    small_refs[:n_small_rows] = [r.at[pl.ds(mixer_layer, 1), :]
                                 for r in small_refs[:n_small_rows]]
    final_norm_ref = take(1)[0] if is_last else None
    next_f32_refs = take(n_next)
    o_ref = take(1)[0]
    next_bf16_refs = take(n_next)
    scratch = refs

    if own_layers is not None:
        w_vmem, (stage_ref, sem_ref) = scratch[-N_BIG - 2:-2], scratch[-2:]
        scratch = scratch[:-N_BIG - 2]

        @pl.when(pl.program_id(0) == 0)
        def _():
            _round_weights_into_vmem(big_refs, own_layers, w_vmem, stage_ref, sem_ref)

        big_refs = w_vmem
    (w_in_ref, w_out_ref, w_gate_ref, w_up_ref, w_down_ref,
     ple_w_gate_ref, ple_w_proj_ref) = big_refs

    carry = {}
    if mixer == "conv":
        carry_ref, = scratch

        @pl.when(pl.program_id(0) % tiles_per_seq == 0)
        def _():
            carry_ref[...] = jnp.zeros_like(carry_ref)

        carry["prev"] = carry_ref[...]

    def row_block(rows):
        h = h_ref[rows, :]
        hn = _rms(h, mix_norm_ref[...]).astype(_MXU_DTYPE)
        d = h.shape[1]
        if mixer == "conv":
            b_gate = _dot(hn, w_in_ref[:, :d])
            z = jnp.concatenate(
                [_dot(hn, w_in_ref[:, d + n:d + n + MXU_COLS])
                 * _dot(hn, w_in_ref[:, 2 * d + n:2 * d + n + MXU_COLS])
                 for n in range(0, d, MXU_COLS)], axis=1)
        else:
            v = _dot(hn, w_in_ref[:, d:])
        yield
        if mixer == "conv":
            gated, carry["prev"] = _conv_gate(b_gate, z, carry["prev"], *small_refs)
        else:
            gated = _sg_gate(hn, w_in_ref, v, *small_refs, *(s.at[rows, :] for s in scratch))
        mix = _dot(gated, w_out_ref[...])
        yield
        h = h + mix
        hn = _rms(h, ffn_norm_ref[...]).astype(_MXU_DTYPE)
        acts = []
        for n in range(w_gate_ref.shape[1] // MXU_COLS):
            cols = slice(n * MXU_COLS, (n + 1) * MXU_COLS)
            ffn_gate = _dot(hn, w_gate_ref[:, cols])
            ffn_up = _dot(hn, w_up_ref[:, cols])
            acts.append((jax.nn.silu(ffn_gate) * ffn_up).astype(_MXU_DTYPE))
        act = jnp.concatenate(acts, axis=1)
        yield
        ffn = _dot(act, w_down_ref[...])
        yield
        h = h + ffn
        hn = _rms(h, ple_norm_ref[...]).astype(_MXU_DTYPE)
        pb = p_ref[rows, :].astype(_MXU_DTYPE)
        h = jnp.concatenate(
            [h[:, n:n + MXU_COLS]
             + jax.nn.sigmoid(_dot(hn, ple_w_gate_ref[:, n:n + MXU_COLS]))
             * _dot(pb, ple_w_proj_ref[:, n:n + MXU_COLS])
             for n in range(0, d, MXU_COLS)], axis=1)
        yield
        if is_last:
            h = _rms(h, final_norm_ref[...])
        o_ref[rows, :] = h

    rb = h_ref.shape[0] // ROW_CHAINS
    blocks = [row_block(pl.ds(c * rb, rb)) for c in range(ROW_CHAINS)]
    while blocks:
        blocks = [b for b in blocks if next(b, StopIteration) is not StopIteration]

    if mixer == "conv":
        carry_ref[...] = carry["prev"]
    for src, dst in zip(next_f32_refs, next_bf16_refs):
        dst[...] = src[...].astype(_MXU_DTYPE)


def _const_spec(shape, index):
    return pl.BlockSpec(shape, lambda i: index, pipeline_mode=pl.Buffered(1))


def _stacked(arr, layer):
    tail = arr.shape[1:]
    return _const_spec((None,) + tail, (layer,) + (0,) * len(tail))


def _row_blocks(rows, n_steps):
    n = n_steps
    while rows % (n * BF16_ROWS):
        assert n % 2 == 0, (rows, n_steps)
        n //= 2
    return n


def _run_layer(h, p, layer, mixer, mixer_layer, big, small, norms, final_norm, next_big, seq_len,
               scratch_shapes):
    t, d = h.shape
    tm = ROW_TILE
    assert t % tm == 0 and seq_len % tm == 0
    n_steps = t // tm
    is_last = final_norm is not None

    operands = [h, p]
    in_specs = [pl.BlockSpec((tm, d), lambda i: (i, 0)),
                pl.BlockSpec((None, tm, p.shape[-1]), lambda i: (layer, i, 0))]
    for arr in norms:
        operands.append(arr)
        in_specs.append(_const_spec(arr.shape, (0, 0)))
    own_layers = None
    if isinstance(big[0], tuple):
        own_layers = tuple(idx for _, idx in big)
        scratch_shapes = list(scratch_shapes)
        scratch_shapes += [pltpu.VMEM(arr.shape[1:], _MXU_DTYPE) for arr, _ in big]
        scratch_shapes += [pltpu.VMEM((STAGE_SLOTS, STAGE_ROWS, STAGE_COLS), jnp.float32),
                           pltpu.SemaphoreType.DMA((STAGE_SLOTS,))]
        for arr, _ in big:
            operands.append(arr)
            in_specs.append(pl.BlockSpec(memory_space=pl.ANY))
    else:
        for arr in big:
            operands.append(arr)
            in_specs.append(_const_spec(arr.shape, (0, 0)))
    n_small_rows = sum(arr.ndim == 2 for arr in small)
    assert all(arr.ndim == 2 for arr in small[:n_small_rows])
    for arr in small:
        operands.append(arr)
        in_specs.append(_const_spec(arr.shape, (0, 0)) if arr.ndim == 2
                        else _stacked(arr, mixer_layer))
    if is_last:
        operands.append(final_norm)
        in_specs.append(_const_spec(final_norm.shape, (0, 0)))

    out_shape = [jax.ShapeDtypeStruct((t, d), jnp.float32)]
    out_specs = [pl.BlockSpec((tm, d), lambda i: (i, 0))]
    for arr, idx in next_big:
        _, rows, cols = arr.shape
        n_blocks = _row_blocks(rows, n_steps)
        per = n_steps // n_blocks
        operands.append(arr)
        in_specs.append(pl.BlockSpec((None, rows // n_blocks, cols),
                                     lambda i, idx=idx, per=per: (idx, i // per, 0)))
        out_shape.append(jax.ShapeDtypeStruct((rows, cols), _MXU_DTYPE))
        out_specs.append(pl.BlockSpec((rows // n_blocks, cols),
                                      lambda i, per=per: (i // per, 0)))

    kernel = functools.partial(_layer_kernel, mixer=mixer, layer=layer, mixer_layer=mixer_layer,
                               n_small=len(small), n_small_rows=n_small_rows, is_last=is_last,
                               n_next=len(next_big), tiles_per_seq=seq_len // tm,
                               own_layers=own_layers)
    outs = pl.pallas_call(
        kernel,
        grid=(n_steps,),
        in_specs=in_specs,
        out_specs=out_specs,
        out_shape=out_shape,
        scratch_shapes=scratch_shapes,
        compiler_params=pltpu.CompilerParams(
            dimension_semantics=("arbitrary",),
            vmem_limit_bytes=VMEM_LIMIT_BYTES),
        name=f"layer{layer}_{mixer}",
    )(*operands)
    return outs[0], list(outs[1:])


def kernel(x, p, mix_norm, conv_w_in, conv_w, conv_w_out, sg_w_in, sg_v_gain, sg_v_bias,
           sg_w_spatial, sg_b_spatial, sg_w_out, ffn_norm, ffn_w_gate, ffn_w_up, ffn_w_down,
           ple_norm, ple_w_gate, ple_w_proj, final_norm):
    bsz, seq_len, d = x.shape
    depth = p.shape[0]
    t = bsz * seq_len
    n_groups, chunk = sg_b_spatial.shape[1:]
    gd = d // n_groups
    assert (ROW_TILE // ROW_CHAINS) % chunk == 0

    norms = (mix_norm, ffn_norm, ple_norm)
    b_s_tile = jnp.repeat(jnp.swapaxes(sg_b_spatial, 1, 2), gd, axis=2)

    def big_f32(layer):
        j = layer // 2
        mix = [(conv_w_in, j), (conv_w_out, j)] if layer % 2 == 0 else [(sg_w_in, j), (sg_w_out, j)]
        return mix + [(w, layer) for w in (ffn_w_gate, ffn_w_up, ffn_w_down, ple_w_gate, ple_w_proj)]

    h = x.reshape(t, d)
    p2 = p.reshape(depth, t, p.shape[-1])
    big = big_f32(0)
    for layer in range(depth):
        last = layer == depth - 1
        fn = final_norm.reshape(1, d) if last else None
        next_big = [] if last else big_f32(layer + 1)
        j = layer // 2
        if layer % 2 == 0:
            mixer, small = "conv", [conv_w]
            scratch = [pltpu.VMEM((SUBLANES, d), jnp.float32)]
        else:
            mixer = "sg"
            small = [sg_v_gain, sg_v_bias, sg_w_spatial, b_s_tile]
            scratch = [pltpu.VMEM((ROW_TILE, d), _MXU_DTYPE),
                       pltpu.VMEM((ROW_TILE, d), _MXU_DTYPE)]
        h, big = _run_layer(h, p2, layer, mixer, j, big, small, norms, fn, next_big, seq_len,
                            scratch)
    return h.reshape(bsz, seq_len, d)
```

```python
import functools

import jax
import jax.numpy as jnp
from jax import lax
from jax.experimental import pallas as pl
from jax.experimental.pallas import tpu as pltpu

RMS_EPS = 1e-6
LN_EPS = 1e-5
SUBLANES = 8
BF16_ROWS = 16
ROW_TILE = 512
ROW_CHAINS = 2
MXU_COLS = 256
STAGE_ROWS, STAGE_COLS = 256, 1024
STAGE_SLOTS = 6
VMEM_LIMIT_BYTES = 58 * 1024 * 1024

_MXU_DTYPE = jnp.bfloat16
N_BIG = 7


def _rms(x, g):
    var = jnp.mean(x * x, axis=-1, keepdims=True)
    return (x * lax.rsqrt(var + RMS_EPS)) * g


def _dot(a, b):
    return jnp.dot(a, b, preferred_element_type=jnp.float32)


def _conv_gate(b_gate, z, prev, w_conv_ref):
    tm, d = z.shape

    w = w_conv_ref[...]
    w0, w1, w2 = w[0:1, :], w[1:2, :], w[2:3, :]
    y = w2 * z + w1 * pltpu.roll(z, 1, axis=0) + w0 * pltpu.roll(z, 2, axis=0)
    row = lax.broadcasted_iota(jnp.int32, (SUBLANES, d), 0)
    zh = z[:SUBLANES, :]
    z1h = jnp.where(row < 1, pltpu.roll(prev, 1, axis=0), pltpu.roll(zh, 1, axis=0))
    z2h = jnp.where(row < 2, pltpu.roll(prev, 2, axis=0), pltpu.roll(zh, 2, axis=0))
    yh = w2 * zh + w1 * z1h + w0 * z2h
    y = jnp.concatenate([yh, y[SUBLANES:, :]], axis=0)
    return (b_gate * y).astype(_MXU_DTYPE), z[tm - SUBLANES:, :]


def _sg_gate(hn, w_in_ref, v, gain_ref, bias_ref, w_s_ref, b_s_ref, v_ref, y_ref):
    tm, d = v.shape
    n_groups, chunk, _ = w_s_ref.shape
    gd = d // n_groups
    n_chunks = tm // chunk
    mu = jnp.mean(v, axis=-1, keepdims=True)
    vc = v - mu
    var = jnp.mean(vc * vc, axis=-1, keepdims=True)
    v = (vc * lax.rsqrt(var + LN_EPS)) * gain_ref[...] + bias_ref[...]
    v_ref[...] = v.astype(_MXU_DTYPE)

    t_idx = lax.broadcasted_iota(jnp.int32, (chunk, chunk), 0)
    s_idx = lax.broadcasted_iota(jnp.int32, (chunk, chunk), 1)
    causal = s_idx <= t_idx
    b_s = b_s_ref[...]
    for g in range(n_groups):
        if g * gd % MXU_COLS == 0:
            u_cols = g * gd
            u = _dot(hn, w_in_ref[:, u_cols:u_cols + MXU_COLS])
        w_g = jnp.where(causal, w_s_ref[g], 0.0).astype(_MXU_DTYPE)
        cols = slice(g * gd, (g + 1) * gd)
        ucols = slice(g * gd - u_cols, (g + 1) * gd - u_cols)
        rhs = jnp.concatenate(
            [v_ref[c * chunk:(c + 1) * chunk, cols] for c in range(n_chunks)], axis=1)
        mixed = _dot(w_g, rhs)
        for c in range(n_chunks):
            rows = slice(c * chunk, (c + 1) * chunk)
            y_ref[rows, cols] = (
                u[rows, ucols] * (mixed[:, c * gd:(c + 1) * gd] + b_s[:, cols])
            ).astype(_MXU_DTYPE)
    return y_ref[...]


def _round_weights_into_vmem(hbm_refs, layers, vmem_refs, stage_ref, sem_ref):
    n_slots = stage_ref.shape[0]
    chunks = []
    for w, dst in enumerate(vmem_refs):
        k, n = dst.shape
        assert k % STAGE_ROWS == 0
        chunks += [(w, r0, c0, min(STAGE_COLS, n - c0))
                   for r0 in range(0, k, STAGE_ROWS) for c0 in range(0, n, STAGE_COLS)]

    def copy(j):
        w, r0, c0, cols = chunks[j]
        return pltpu.make_async_copy(
            hbm_refs[w].at[layers[w], pl.ds(r0, STAGE_ROWS), pl.ds(c0, cols)],
            stage_ref.at[j % n_slots, :, pl.ds(0, cols)],
            sem_ref.at[j % n_slots])

    for j in range(min(n_slots - 1, len(chunks))):
        copy(j).start()
    for j, (w, r0, c0, cols) in enumerate(chunks):
        ahead = j + n_slots - 1
        if ahead < len(chunks):
            copy(ahead).start()
        copy(j).wait()
        vmem_refs[w][pl.ds(r0, STAGE_ROWS), pl.ds(c0, cols)] = (
            stage_ref[j % n_slots, :, pl.ds(0, cols)].astype(_MXU_DTYPE))


def _layer_kernel(*refs, mixer, layer, mixer_layer, n_small, n_small_rows, is_last, n_next,
                  tiles_per_seq, own_layers):
    refs = list(refs)
    take = lambda n: [refs.pop(0) for _ in range(n)]
    h_ref, p_ref = take(2)
    mix_norm_ref, ffn_norm_ref, ple_norm_ref = (r.at[pl.ds(layer, 1), :] for r in take(3))
    big_refs = take(N_BIG)
    small_refs = take(n_small)
    small_refs[:n_small_rows] = [r.at[pl.ds(mixer_layer, 1), :]
                                 for r in small_refs[:n_small_rows]]
    final_norm_ref = take(1)[0] if is_last else None
    next_f32_refs = take(n_next)
    o_ref = take(1)[0]
    next_bf16_refs = take(n_next)
    scratch = refs

    if own_layers is not None:
        w_vmem, (stage_ref, sem_ref) = scratch[-N_BIG - 2:-2], scratch[-2:]
        scratch = scratch[:-N_BIG - 2]

        @pl.when(pl.program_id(0) == 0)
        def _():
            _round_weights_into_vmem(big_refs, own_layers, w_vmem, stage_ref, sem_ref)

        big_refs = w_vmem
    (w_in_ref, w_out_ref, w_gate_ref, w_up_ref, w_down_ref,
     ple_w_gate_ref, ple_w_proj_ref) = big_refs

    carry = {}
    if mixer == "conv":
        carry_ref, = scratch

        @pl.when(pl.program_id(0) % tiles_per_seq == 0)
        def _():
            carry_ref[...] = jnp.zeros_like(carry_ref)

        carry["prev"] = carry_ref[...]

    def row_block(rows):
        h = h_ref[rows, :]
        hn = _rms(h, mix_norm_ref[...]).astype(_MXU_DTYPE)
        d = h.shape[1]
        if mixer == "conv":
            b_gate = _dot(hn, w_in_ref[:, :d])
            z = jnp.concatenate(
                [_dot(hn, w_in_ref[:, d + n:d + n + MXU_COLS])
                 * _dot(hn, w_in_ref[:, 2 * d + n:2 * d + n + MXU_COLS])
                 for n in range(0, d, MXU_COLS)], axis=1)
        else:
            v = _dot(hn, w_in_ref[:, d:])
        yield
        if mixer == "conv":
            gated, carry["prev"] = _conv_gate(b_gate, z, carry["prev"], *small_refs)
        else:
            gated = _sg_gate(hn, w_in_ref, v, *small_refs, *(s.at[rows, :] for s in scratch))
        mix = _dot(gated, w_out_ref[...])
        yield
        h = h + mix
        hn = _rms(h, ffn_norm_ref[...]).astype(_MXU_DTYPE)
        acts = []
        for n in range(w_gate_ref.shape[1] // MXU_COLS):
            cols = slice(n * MXU_COLS, (n + 1) * MXU_COLS)
            ffn_gate = _dot(hn, w_gate_ref[:, cols])
            ffn_up = _dot(hn, w_up_ref[:, cols])
            acts.append((jax.nn.silu(ffn_gate) * ffn_up).astype(_MXU_DTYPE))
        act = jnp.concatenate(acts, axis=1)
        yield
        ffn = _dot(act, w_down_ref[...])
        yield
        h = h + ffn
        hn = _rms(h, ple_norm_ref[...]).astype(_MXU_DTYPE)
        pb = p_ref[rows, :].astype(_MXU_DTYPE)
        h = jnp.concatenate(
            [h[:, n:n + MXU_COLS]
             + jax.nn.sigmoid(_dot(hn, ple_w_gate_ref[:, n:n + MXU_COLS]))
             * _dot(pb, ple_w_proj_ref[:, n:n + MXU_COLS])
             for n in range(0, d, MXU_COLS)], axis=1)
        yield
        if is_last:
            h = _rms(h, final_norm_ref[...])
        o_ref[rows, :] = h

    rb = h_ref.shape[0] // ROW_CHAINS
    blocks = [row_block(pl.ds(c * rb, rb)) for c in range(ROW_CHAINS)]
    while blocks:
        blocks = [b for b in blocks if next(b, StopIteration) is not StopIteration]

    if mixer == "conv":
        carry_ref[...] = carry["prev"]
    for src, dst in zip(next_f32_refs, next_bf16_refs):
        dst[...] = src[...].astype(_MXU_DTYPE)


def _const_spec(shape, index):
    return pl.BlockSpec(shape, lambda i: index, pipeline_mode=pl.Buffered(1))


def _stacked(arr, layer):
    tail = arr.shape[1:]
    return _const_spec((None,) + tail, (layer,) + (0,) * len(tail))


def _row_blocks(rows, n_steps):
    n = n_steps
    while rows % (n * BF16_ROWS):
        assert n % 2 == 0, (rows, n_steps)
        n //= 2
    return n


def _run_layer(h, p, layer, mixer, mixer_layer, big, small, norms, final_norm, next_big, seq_len,
               scratch_shapes):
    t, d = h.shape
    tm = ROW_TILE
    assert t % tm == 0 and seq_len % tm == 0
    n_steps = t // tm
    is_last = final_norm is not None

    operands = [h, p]
    in_specs = [pl.BlockSpec((tm, d), lambda i: (i, 0)),
                pl.BlockSpec((None, tm, p.shape[-1]), lambda i: (layer, i, 0))]
    for arr in norms:
        operands.append(arr)
        in_specs.append(_const_spec(arr.shape, (0, 0)))
    own_layers = None
    if isinstance(big[0], tuple):
        own_layers = tuple(idx for _, idx in big)
        scratch_shapes = list(scratch_shapes)
        scratch_shapes += [pltpu.VMEM(arr.shape[1:], _MXU_DTYPE) for arr, _ in big]
        scratch_shapes += [pltpu.VMEM((STAGE_SLOTS, STAGE_ROWS, STAGE_COLS), jnp.float32),
                           pltpu.SemaphoreType.DMA((STAGE_SLOTS,))]
        for arr, _ in big:
            operands.append(arr)
            in_specs.append(pl.BlockSpec(memory_space=pl.ANY))
    else:
        for arr in big:
            operands.append(arr)
            in_specs.append(_const_spec(arr.shape, (0, 0)))
    n_small_rows = sum(arr.ndim == 2 for arr in small)
    assert all(arr.ndim == 2 for arr in small[:n_small_rows])
    for arr in small:
        operands.append(arr)
        in_specs.append(_const_spec(arr.shape, (0, 0)) if arr.ndim == 2
                        else _stacked(arr, mixer_layer))
    if is_last:
        operands.append(final_norm)
        in_specs.append(_const_spec(final_norm.shape, (0, 0)))

    out_shape = [jax.ShapeDtypeStruct((t, d), jnp.float32)]
    out_specs = [pl.BlockSpec((tm, d), lambda i: (i, 0))]
    for arr, idx in next_big:
        _, rows, cols = arr.shape
        n_blocks = _row_blocks(rows, n_steps)
        per = n_steps // n_blocks
        operands.append(arr)
        in_specs.append(pl.BlockSpec((None, rows // n_blocks, cols),
                                     lambda i, idx=idx, per=per: (idx, i // per, 0)))
        out_shape.append(jax.ShapeDtypeStruct((rows, cols), _MXU_DTYPE))
        out_specs.append(pl.BlockSpec((rows // n_blocks, cols),
                                      lambda i, per=per: (i // per, 0)))

    kernel = functools.partial(_layer_kernel, mixer=mixer, layer=layer, mixer_layer=mixer_layer,
                               n_small=len(small), n_small_rows=n_small_rows, is_last=is_last,
                               n_next=len(next_big), tiles_per_seq=seq_len // tm,
                               own_layers=own_layers)
    outs = pl.pallas_call(
        kernel,
        grid=(n_steps,),
        in_specs=in_specs,
        out_specs=out_specs,
        out_shape=out_shape,
        scratch_shapes=scratch_shapes,
        compiler_params=pltpu.CompilerParams(
            dimension_semantics=("arbitrary",),
            vmem_limit_bytes=VMEM_LIMIT_BYTES),
        name=f"layer{layer}_{mixer}",
    )(*operands)
    return outs[0], list(outs[1:])


def kernel(x, p, mix_norm, conv_w_in, conv_w, conv_w_out, sg_w_in, sg_v_gain, sg_v_bias,
           sg_w_spatial, sg_b_spatial, sg_w_out, ffn_norm, ffn_w_gate, ffn_w_up, ffn_w_down,
           ple_norm, ple_w_gate, ple_w_proj, final_norm):
    bsz, seq_len, d = x.shape
    depth = p.shape[0]
    t = bsz * seq_len
    n_groups, chunk = sg_b_spatial.shape[1:]
    gd = d // n_groups
    rb = ROW_TILE // ROW_CHAINS
    assert ROW_TILE % ROW_CHAINS == 0 and rb % chunk == 0 and rb % BF16_ROWS == 0
    assert d % MXU_COLS == 0 and ffn_w_gate.shape[-1] % MXU_COLS == 0 and MXU_COLS % gd == 0
    assert d % n_groups == 0 and sg_w_spatial.shape[-2:] == (chunk, chunk)

    norms = (mix_norm, ffn_norm, ple_norm)
    b_s_tile = jnp.repeat(jnp.swapaxes(sg_b_spatial, 1, 2), gd, axis=2)

    def big_f32(layer):
        j = layer // 2
        mix = [(conv_w_in, j), (conv_w_out, j)] if layer % 2 == 0 else [(sg_w_in, j), (sg_w_out, j)]
        return mix + [(w, layer) for w in (ffn_w_gate, ffn_w_up, ffn_w_down, ple_w_gate, ple_w_proj)]

    h = x.reshape(t, d)
    p2 = p.reshape(depth, t, p.shape[-1])
    big = big_f32(0)
    for layer in range(depth):
        last = layer == depth - 1
        fn = final_norm.reshape(1, d) if last else None
        next_big = [] if last else big_f32(layer + 1)
        j = layer // 2
        if layer % 2 == 0:
            mixer, small = "conv", [conv_w]
            scratch = [pltpu.VMEM((SUBLANES, d), jnp.float32)]
        else:
            mixer = "sg"
            small = [sg_v_gain, sg_v_bias, sg_w_spatial, b_s_tile]
            scratch = [pltpu.VMEM((ROW_TILE, d), _MXU_DTYPE),
                       pltpu.VMEM((ROW_TILE, d), _MXU_DTYPE)]
        h, big = _run_layer(h, p2, layer, mixer, j, big, small, norms, fn, next_big, seq_len,
                            scratch)
    return h.reshape(bsz, seq_len, d)
```

```python
import functools

import jax
import jax.numpy as jnp
from jax import lax
from jax.experimental import pallas as pl
from jax.experimental.pallas import tpu as pltpu

RMS_EPS = 1e-6
LN_EPS = 1e-5
SUBLANES = 8
BF16_ROWS = 16
ROW_TILE = 512
ROW_CHAINS = 2
MXU_COLS = 256
STAGE_ROWS, STAGE_COLS = 256, 1024
STAGE_SLOTS = 6
COPY_SLOTS = 3
VMEM_LIMIT_BYTES = 58 * 1024 * 1024

_MXU_DTYPE = jnp.bfloat16
N_BIG = 7


def _rms(x, g):
    var = jnp.mean(x * x, axis=-1, keepdims=True)
    return (x * lax.rsqrt(var + RMS_EPS)) * g


def _dot(a, b):
    return jnp.dot(a, b, preferred_element_type=jnp.float32)


def _conv_gate(b_gate, z, prev, w_conv_ref):
    tm, d = z.shape

    w = w_conv_ref[...]
    w0, w1, w2 = w[0:1, :], w[1:2, :], w[2:3, :]
    y = w2 * z + w1 * pltpu.roll(z, 1, axis=0) + w0 * pltpu.roll(z, 2, axis=0)
    row = lax.broadcasted_iota(jnp.int32, (SUBLANES, d), 0)
    zh = z[:SUBLANES, :]
    z1h = jnp.where(row < 1, pltpu.roll(prev, 1, axis=0), pltpu.roll(zh, 1, axis=0))
    z2h = jnp.where(row < 2, pltpu.roll(prev, 2, axis=0), pltpu.roll(zh, 2, axis=0))
    yh = w2 * zh + w1 * z1h + w0 * z2h
    y = jnp.concatenate([yh, y[SUBLANES:, :]], axis=0)
    return (b_gate * y).astype(_MXU_DTYPE), z[tm - SUBLANES:, :]


def _sg_gate(hn, w_in_ref, v, gain_ref, bias_ref, w_s_ref, b_s_ref, v_ref, y_ref):
    tm, d = v.shape
    n_groups, chunk, _ = w_s_ref.shape
    gd = d // n_groups
    n_chunks = tm // chunk
    mu = jnp.mean(v, axis=-1, keepdims=True)
    vc = v - mu
    var = jnp.mean(vc * vc, axis=-1, keepdims=True)
    v = (vc * lax.rsqrt(var + LN_EPS)) * gain_ref[...] + bias_ref[...]
    v_ref[...] = v.astype(_MXU_DTYPE)

    t_idx = lax.broadcasted_iota(jnp.int32, (chunk, chunk), 0)
    s_idx = lax.broadcasted_iota(jnp.int32, (chunk, chunk), 1)
    causal = s_idx <= t_idx
    b_s = b_s_ref[...]
    for g in range(n_groups):
        if g * gd % MXU_COLS == 0:
            u_cols = g * gd
            u = _dot(hn, w_in_ref[:, u_cols:u_cols + MXU_COLS])
        w_g = jnp.where(causal, w_s_ref[g], 0.0).astype(_MXU_DTYPE)
        cols = slice(g * gd, (g + 1) * gd)
        ucols = slice(g * gd - u_cols, (g + 1) * gd - u_cols)
        rhs = jnp.concatenate(
            [v_ref[c * chunk:(c + 1) * chunk, cols] for c in range(n_chunks)], axis=1)
        mixed = _dot(w_g, rhs)
        for c in range(n_chunks):
            rows = slice(c * chunk, (c + 1) * chunk)
            y_ref[rows, cols] = (
                u[rows, ucols] * (mixed[:, c * gd:(c + 1) * gd] + b_s[:, cols])
            ).astype(_MXU_DTYPE)
    return y_ref[...]


class _WeightFeed:
    def __init__(self, hbm_refs, layers, vmem_refs, stage_ref, sem_ref):
        self.hbm_refs, self.layers, self.vmem_refs = hbm_refs, layers, vmem_refs
        self.stage_ref, self.sem_ref = stage_ref, sem_ref
        self.n_slots = sem_ref.shape[0]
        self.chunks = []
        for w, dst in enumerate(vmem_refs):
            k, n = dst.shape
            if layers is None:
                self.chunks.append((w, 0, k, 0, n))
            else:
                assert k % STAGE_ROWS == 0
                self.chunks += [(w, r0, STAGE_ROWS, c0, min(STAGE_COLS, n - c0))
                                for r0 in range(0, k, STAGE_ROWS)
                                for c0 in range(0, n, STAGE_COLS)]
        self.done = 0
        for j in range(min(self.n_slots - 1, len(self.chunks))):
            self._copy(j).start()

    def _copy(self, j):
        w, r0, rows, c0, cols = self.chunks[j]
        slot = j % self.n_slots
        if self.layers is None:
            src, dst = self.hbm_refs[w], self.vmem_refs[w]
        else:
            src = self.hbm_refs[w].at[self.layers[w], pl.ds(r0, rows), pl.ds(c0, cols)]
            dst = self.stage_ref.at[slot, :, pl.ds(0, cols)]
        return pltpu.make_async_copy(src, dst, self.sem_ref.at[slot])

    def ensure(self, weight):
        while self.done < len(self.chunks) and self.chunks[self.done][0] <= weight:
            j = self.done
            w, r0, rows, c0, cols = self.chunks[j]
            ahead = j + self.n_slots - 1
            if ahead < len(self.chunks):
                self._copy(ahead).start()
            self._copy(j).wait()
            if self.layers is not None:
                self.vmem_refs[w][pl.ds(r0, rows), pl.ds(c0, cols)] = (
                    self.stage_ref[j % self.n_slots, :, pl.ds(0, cols)].astype(_MXU_DTYPE))
            self.done += 1


W_IN, W_OUT, W_GATE, W_UP, W_DOWN, PLE_W_GATE, PLE_W_PROJ = range(N_BIG)


def _layer_kernel(*refs, mixer, layer, mixer_layer, n_small, n_small_rows, is_last, n_next,
                  tiles_per_seq, own_layers):
    refs = list(refs)
    take = lambda n: [refs.pop(0) for _ in range(n)]
    h_ref, p_ref = take(2)
    mix_norm_ref, ffn_norm_ref, ple_norm_ref = (r.at[pl.ds(layer, 1), :] for r in take(3))
    hbm_w_refs = take(N_BIG)
    small_refs = take(n_small)
    small_refs[:n_small_rows] = [r.at[pl.ds(mixer_layer, 1), :]
                                 for r in small_refs[:n_small_rows]]
    final_norm_ref = take(1)[0] if is_last else None
    next_f32_refs = take(n_next)
    o_ref = take(1)[0]
    next_bf16_refs = take(n_next)
    sem_ref = refs.pop()
    stage_ref = refs.pop() if own_layers is not None else None
    scratch, w_refs = refs[:-N_BIG], refs[-N_BIG:]
    (w_in_ref, w_out_ref, w_gate_ref, w_up_ref, w_down_ref,
     ple_w_gate_ref, ple_w_proj_ref) = w_refs

    def step(feed):
        need = feed.ensure if feed is not None else lambda weight: None
        carry = {}
        if mixer == "conv":
            carry_ref, = scratch

            @pl.when(pl.program_id(0) % tiles_per_seq == 0)
            def _():
                carry_ref[...] = jnp.zeros_like(carry_ref)

            carry["prev"] = carry_ref[...]

        def row_block(rows):
            h = h_ref[rows, :]
            hn = _rms(h, mix_norm_ref[...]).astype(_MXU_DTYPE)
            d = h.shape[1]
            need(W_IN)
            if mixer == "conv":
                b_gate = _dot(hn, w_in_ref[:, :d])
                z = jnp.concatenate(
                    [_dot(hn, w_in_ref[:, d + n:d + n + MXU_COLS])
                     * _dot(hn, w_in_ref[:, 2 * d + n:2 * d + n + MXU_COLS])
                     for n in range(0, d, MXU_COLS)], axis=1)
            else:
                v = _dot(hn, w_in_ref[:, d:])
            yield
            if mixer == "conv":
                gated, carry["prev"] = _conv_gate(b_gate, z, carry["prev"], *small_refs)
            else:
                gated = _sg_gate(hn, w_in_ref, v, *small_refs,
                                 *(s.at[rows, :] for s in scratch))
            need(W_OUT)
            mix = _dot(gated, w_out_ref[...])
            yield
            h = h + mix
            hn = _rms(h, ffn_norm_ref[...]).astype(_MXU_DTYPE)
            need(W_UP)
            acts = []
            for n in range(w_gate_ref.shape[1] // MXU_COLS):
                cols = slice(n * MXU_COLS, (n + 1) * MXU_COLS)
                ffn_gate = _dot(hn, w_gate_ref[:, cols])
                ffn_up = _dot(hn, w_up_ref[:, cols])
                acts.append((jax.nn.silu(ffn_gate) * ffn_up).astype(_MXU_DTYPE))
            act = jnp.concatenate(acts, axis=1)
            yield
            need(W_DOWN)
            ffn = _dot(act, w_down_ref[...])
            yield
            h = h + ffn
            hn = _rms(h, ple_norm_ref[...]).astype(_MXU_DTYPE)
            pb = p_ref[rows, :].astype(_MXU_DTYPE)
            need(PLE_W_PROJ)
            h = jnp.concatenate(
                [h[:, n:n + MXU_COLS]
                 + jax.nn.sigmoid(_dot(hn, ple_w_gate_ref[:, n:n + MXU_COLS]))
                 * _dot(pb, ple_w_proj_ref[:, n:n + MXU_COLS])
                 for n in range(0, d, MXU_COLS)], axis=1)
            yield
            if is_last:
                h = _rms(h, final_norm_ref[...])
            o_ref[rows, :] = h

        rb = h_ref.shape[0] // ROW_CHAINS
        blocks = [row_block(pl.ds(c * rb, rb)) for c in range(ROW_CHAINS)]
        while blocks:
            blocks = [b for b in blocks if next(b, StopIteration) is not StopIteration]

        if mixer == "conv":
            carry_ref[...] = carry["prev"]
        for src, dst in zip(next_f32_refs, next_bf16_refs):
            dst[...] = src[...].astype(_MXU_DTYPE)

    @pl.when(pl.program_id(0) == 0)
    def _():
        feed = _WeightFeed(hbm_w_refs, own_layers, w_refs, stage_ref, sem_ref)
        step(feed)
        feed.ensure(N_BIG - 1)

    @pl.when(pl.program_id(0) > 0)
    def _():
        step(None)


def _const_spec(shape, index):
    return pl.BlockSpec(shape, lambda i: index, pipeline_mode=pl.Buffered(1))


def _stacked(arr, layer):
    tail = arr.shape[1:]
    return _const_spec((None,) + tail, (layer,) + (0,) * len(tail))


def _row_blocks(rows, n_steps):
    n = n_steps
    while rows % (n * BF16_ROWS):
        assert n % 2 == 0, (rows, n_steps)
        n //= 2
    return n


def _run_layer(h, p, layer, mixer, mixer_layer, big, small, norms, final_norm, next_big, seq_len,
               scratch_shapes):
    t, d = h.shape
    tm = ROW_TILE
    assert t % tm == 0 and seq_len % tm == 0
    n_steps = t // tm
    is_last = final_norm is not None

    operands = [h, p]
    in_specs = [pl.BlockSpec((tm, d), lambda i: (i, 0)),
                pl.BlockSpec((None, tm, p.shape[-1]), lambda i: (layer, i, 0))]
    for arr in norms:
        operands.append(arr)
        in_specs.append(_const_spec(arr.shape, (0, 0)))
    scratch_shapes = list(scratch_shapes)
    if isinstance(big[0], tuple):
        own_layers = tuple(idx for _, idx in big)
        big = [arr for arr, _ in big]
        scratch_shapes += [pltpu.VMEM(arr.shape[1:], _MXU_DTYPE) for arr in big]
        scratch_shapes += [pltpu.VMEM((STAGE_SLOTS, STAGE_ROWS, STAGE_COLS), jnp.float32),
                           pltpu.SemaphoreType.DMA((STAGE_SLOTS,))]
    else:
        own_layers = None
        scratch_shapes += [pltpu.VMEM(arr.shape, _MXU_DTYPE) for arr in big]
        scratch_shapes += [pltpu.SemaphoreType.DMA((COPY_SLOTS,))]
    for arr in big:
        operands.append(arr)
        in_specs.append(pl.BlockSpec(memory_space=pl.ANY))
    n_small_rows = sum(arr.ndim == 2 for arr in small)
    assert all(arr.ndim == 2 for arr in small[:n_small_rows])
    for arr in small:
        operands.append(arr)
        in_specs.append(_const_spec(arr.shape, (0, 0)) if arr.ndim == 2
                        else _stacked(arr, mixer_layer))
    if is_last:
        operands.append(final_norm)
        in_specs.append(_const_spec(final_norm.shape, (0, 0)))

    out_shape = [jax.ShapeDtypeStruct((t, d), jnp.float32)]
    out_specs = [pl.BlockSpec((tm, d), lambda i: (i, 0))]
    for arr, idx in next_big:
        _, rows, cols = arr.shape
        n_blocks = _row_blocks(rows, n_steps)
        per = n_steps // n_blocks
        operands.append(arr)
        in_specs.append(pl.BlockSpec((None, rows // n_blocks, cols),
                                     lambda i, idx=idx, per=per: (idx, i // per, 0)))
        out_shape.append(jax.ShapeDtypeStruct((rows, cols), _MXU_DTYPE))
        out_specs.append(pl.BlockSpec((rows // n_blocks, cols),
                                      lambda i, per=per: (i // per, 0)))

    kernel = functools.partial(_layer_kernel, mixer=mixer, layer=layer, mixer_layer=mixer_layer,
                               n_small=len(small), n_small_rows=n_small_rows, is_last=is_last,
                               n_next=len(next_big), tiles_per_seq=seq_len // tm,
                               own_layers=own_layers)
    outs = pl.pallas_call(
        kernel,
        grid=(n_steps,),
        in_specs=in_specs,
        out_specs=out_specs,
        out_shape=out_shape,
        scratch_shapes=scratch_shapes,
        compiler_params=pltpu.CompilerParams(
            dimension_semantics=("arbitrary",),
            vmem_limit_bytes=VMEM_LIMIT_BYTES),
        name=f"layer{layer}_{mixer}",
    )(*operands)
    return outs[0], list(outs[1:])


def kernel(x, p, mix_norm, conv_w_in, conv_w, conv_w_out, sg_w_in, sg_v_gain, sg_v_bias,
           sg_w_spatial, sg_b_spatial, sg_w_out, ffn_norm, ffn_w_gate, ffn_w_up, ffn_w_down,
           ple_norm, ple_w_gate, ple_w_proj, final_norm):
    bsz, seq_len, d = x.shape
    depth = p.shape[0]
    t = bsz * seq_len
    n_groups, chunk = sg_b_spatial.shape[1:]
    gd = d // n_groups
    rb = ROW_TILE // ROW_CHAINS
    assert ROW_TILE % ROW_CHAINS == 0 and rb % chunk == 0 and rb % BF16_ROWS == 0
    assert d % MXU_COLS == 0 and ffn_w_gate.shape[-1] % MXU_COLS == 0 and MXU_COLS % gd == 0
    assert d % n_groups == 0 and sg_w_spatial.shape[-2:] == (chunk, chunk)

    norms = (mix_norm, ffn_norm, ple_norm)
    b_s_tile = jnp.repeat(jnp.swapaxes(sg_b_spatial, 1, 2), gd, axis=2)

    def big_f32(layer):
        j = layer // 2
        mix = [(conv_w_in, j), (conv_w_out, j)] if layer % 2 == 0 else [(sg_w_in, j), (sg_w_out, j)]
        return mix + [(w, layer) for w in (ffn_w_gate, ffn_w_up, ffn_w_down, ple_w_gate, ple_w_proj)]

    h = x.reshape(t, d)
    p2 = p.reshape(depth, t, p.shape[-1])
    big = big_f32(0)
    for layer in range(depth):
        last = layer == depth - 1
        fn = final_norm.reshape(1, d) if last else None
        next_big = [] if last else big_f32(layer + 1)
        j = layer // 2
        if layer % 2 == 0:
            mixer, small = "conv", [conv_w]
            scratch = [pltpu.VMEM((SUBLANES, d), jnp.float32)]
        else:
            mixer = "sg"
            small = [sg_v_gain, sg_v_bias, sg_w_spatial, b_s_tile]
            scratch = [pltpu.VMEM((ROW_TILE, d), _MXU_DTYPE),
                       pltpu.VMEM((ROW_TILE, d), _MXU_DTYPE)]
        h, big = _run_layer(h, p2, layer, mixer, j, big, small, norms, fn, next_big, seq_len,
                            scratch)
    return h.reshape(bsz, seq_len, d)
```

```python
import functools

import jax
import jax.numpy as jnp
from jax import lax
from jax.experimental import pallas as pl
from jax.experimental.pallas import tpu as pltpu

RMS_EPS = 1e-6
LN_EPS = 1e-5
SUBLANES = 8
BF16_ROWS = 16
ROW_TILE = 512
ROW_CHAINS = 2
MXU_COLS = 256
STAGE_ROWS, STAGE_COLS = 256, 1024
STAGE_SLOTS = 6
VMEM_LIMIT_BYTES = 58 * 1024 * 1024

_MXU_DTYPE = jnp.bfloat16
N_BIG = 7


def _rms(x, g):
    var = jnp.mean(x * x, axis=-1, keepdims=True)
    return (x * lax.rsqrt(var + RMS_EPS)) * g


def _dot(a, b):
    return jnp.dot(a, b, preferred_element_type=jnp.float32)


def _conv_gate(b_gate, z, prev, w_conv_ref):
    tm, d = z.shape

    w = w_conv_ref[...]
    w0, w1, w2 = w[0:1, :], w[1:2, :], w[2:3, :]
    y = w2 * z + w1 * pltpu.roll(z, 1, axis=0) + w0 * pltpu.roll(z, 2, axis=0)
    row = lax.broadcasted_iota(jnp.int32, (SUBLANES, d), 0)
    zh = z[:SUBLANES, :]
    z1h = jnp.where(row < 1, pltpu.roll(prev, 1, axis=0), pltpu.roll(zh, 1, axis=0))
    z2h = jnp.where(row < 2, pltpu.roll(prev, 2, axis=0), pltpu.roll(zh, 2, axis=0))
    yh = w2 * zh + w1 * z1h + w0 * z2h
    y = jnp.concatenate([yh, y[SUBLANES:, :]], axis=0)
    return (b_gate * y).astype(_MXU_DTYPE), z[tm - SUBLANES:, :]


def _sg_gate(hn, w_in_ref, v, gain_ref, bias_ref, w_s_ref, b_s_ref, v_ref, y_ref):
    tm, d = v.shape
    n_groups, chunk, _ = w_s_ref.shape
    gd = d // n_groups
    n_chunks = tm // chunk
    mu = jnp.mean(v, axis=-1, keepdims=True)
    vc = v - mu
    var = jnp.mean(vc * vc, axis=-1, keepdims=True)
    v = (vc * lax.rsqrt(var + LN_EPS)) * gain_ref[...] + bias_ref[...]
    v_ref[...] = v.astype(_MXU_DTYPE)

    t_idx = lax.broadcasted_iota(jnp.int32, (chunk, chunk), 0)
    s_idx = lax.broadcasted_iota(jnp.int32, (chunk, chunk), 1)
    causal = s_idx <= t_idx
    b_s = b_s_ref[...]
    for g in range(n_groups):
        if g * gd % MXU_COLS == 0:
            u_cols = g * gd
            u = _dot(hn, w_in_ref[:, u_cols:u_cols + MXU_COLS])
        w_g = jnp.where(causal, w_s_ref[g], 0.0).astype(_MXU_DTYPE)
        cols = slice(g * gd, (g + 1) * gd)
        ucols = slice(g * gd - u_cols, (g + 1) * gd - u_cols)
        rhs = jnp.concatenate(
            [v_ref[c * chunk:(c + 1) * chunk, cols] for c in range(n_chunks)], axis=1)
        mixed = _dot(w_g, rhs)
        for c in range(n_chunks):
            rows = slice(c * chunk, (c + 1) * chunk)
            y_ref[rows, cols] = (
                u[rows, ucols] * (mixed[:, c * gd:(c + 1) * gd] + b_s[:, cols])
            ).astype(_MXU_DTYPE)
    return y_ref[...]


def _round_weights_into_vmem(hbm_refs, layers, vmem_refs, stage_ref, sem_ref):
    n_slots = stage_ref.shape[0]
    chunks = []
    for w, dst in enumerate(vmem_refs):
        k, n = dst.shape
        assert k % STAGE_ROWS == 0
        chunks += [(w, r0, c0, min(STAGE_COLS, n - c0))
                   for r0 in range(0, k, STAGE_ROWS) for c0 in range(0, n, STAGE_COLS)]

    def copy(j):
        w, r0, c0, cols = chunks[j]
        return pltpu.make_async_copy(
            hbm_refs[w].at[layers[w], pl.ds(r0, STAGE_ROWS), pl.ds(c0, cols)],
            stage_ref.at[j % n_slots, :, pl.ds(0, cols)],
            sem_ref.at[j % n_slots])

    for j in range(min(n_slots - 1, len(chunks))):
        copy(j).start()
    for j, (w, r0, c0, cols) in enumerate(chunks):
        ahead = j + n_slots - 1
        if ahead < len(chunks):
            copy(ahead).start()
        copy(j).wait()
        vmem_refs[w][pl.ds(r0, STAGE_ROWS), pl.ds(c0, cols)] = (
            stage_ref[j % n_slots, :, pl.ds(0, cols)].astype(_MXU_DTYPE))


def _layer_kernel(*refs, mixer, layer, mixer_layer, n_small, n_small_rows, is_last, cast_idx,
                  cast_blocks, n_steps, tiles_per_seq, own_layers):
    refs = list(refs)
    take = lambda n: [refs.pop(0) for _ in range(n)]
    h_hbm, p_hbm = take(2)
    mix_norm_ref, ffn_norm_ref, ple_norm_ref = (r.at[pl.ds(layer, 1), :] for r in take(3))
    hbm_w_refs = take(N_BIG)
    small_refs = take(n_small)
    small_refs = [r.at[pl.ds(mixer_layer, 1), :] if k < n_small_rows else r.at[mixer_layer]
                  for k, r in enumerate(small_refs)]
    final_norm_ref = take(1)[0] if is_last else None
    n_next = len(cast_idx)
    next_f32_hbm = take(n_next)
    o_hbm = take(1)[0]
    next_bf16_hbm = take(n_next)
    scratch = refs

    if own_layers is not None:
        w_refs, (stage_ref, sem_ref) = scratch[-N_BIG - 2:-2], scratch[-2:]
        scratch = scratch[:-N_BIG - 2]
        _round_weights_into_vmem(hbm_w_refs, own_layers, w_refs, stage_ref, sem_ref)
    else:
        w_refs = scratch[-N_BIG:]
        scratch = scratch[:-N_BIG]
        for src, dst in zip(hbm_w_refs, w_refs):
            pltpu.sync_copy(src, dst)
    (w_in_ref, w_out_ref, w_gate_ref, w_up_ref, w_down_ref,
     ple_w_gate_ref, ple_w_proj_ref) = w_refs

    def step(indices, h_ref, p_ref, *rest):
        (i,) = indices
        next_f32_refs, o_ref, next_bf16_refs = rest[:n_next], rest[n_next], rest[n_next + 1:]
        carry = {}
        if mixer == "conv":
            carry_ref, = scratch

            @pl.when(i % tiles_per_seq == 0)
            def _():
                carry_ref[...] = jnp.zeros_like(carry_ref)

            carry["prev"] = carry_ref[...]

        def row_block(rows):
            h = h_ref[rows, :]
            hn = _rms(h, mix_norm_ref[...]).astype(_MXU_DTYPE)
            d = h.shape[1]
            if mixer == "conv":
                b_gate = _dot(hn, w_in_ref[:, :d])
                z = jnp.concatenate(
                    [_dot(hn, w_in_ref[:, d + n:d + n + MXU_COLS])
                     * _dot(hn, w_in_ref[:, 2 * d + n:2 * d + n + MXU_COLS])
                     for n in range(0, d, MXU_COLS)], axis=1)
            else:
                v = _dot(hn, w_in_ref[:, d:])
            yield
            if mixer == "conv":
                gated, carry["prev"] = _conv_gate(b_gate, z, carry["prev"], *small_refs)
            else:
                gated = _sg_gate(hn, w_in_ref, v, *small_refs,
                                 *(s.at[rows, :] for s in scratch))
            mix = _dot(gated, w_out_ref[...])
            yield
            h = h + mix
            hn = _rms(h, ffn_norm_ref[...]).astype(_MXU_DTYPE)
            acts = []
            for n in range(w_gate_ref.shape[1] // MXU_COLS):
                cols = slice(n * MXU_COLS, (n + 1) * MXU_COLS)
                ffn_gate = _dot(hn, w_gate_ref[:, cols])
                ffn_up = _dot(hn, w_up_ref[:, cols])
                acts.append((jax.nn.silu(ffn_gate) * ffn_up).astype(_MXU_DTYPE))
            act = jnp.concatenate(acts, axis=1)
            yield
            ffn = _dot(act, w_down_ref[...])
            yield
            h = h + ffn
            hn = _rms(h, ple_norm_ref[...]).astype(_MXU_DTYPE)
            pb = p_ref[rows, :].astype(_MXU_DTYPE)
            h = jnp.concatenate(
                [h[:, n:n + MXU_COLS]
                 + jax.nn.sigmoid(_dot(hn, ple_w_gate_ref[:, n:n + MXU_COLS]))
                 * _dot(pb, ple_w_proj_ref[:, n:n + MXU_COLS])
                 for n in range(0, d, MXU_COLS)], axis=1)
            yield
            if is_last:
                h = _rms(h, final_norm_ref[...])
            o_ref[rows, :] = h

        rb = h_ref.shape[0] // ROW_CHAINS
        blocks = [row_block(pl.ds(c * rb, rb)) for c in range(ROW_CHAINS)]
        while blocks:
            blocks = [b for b in blocks if next(b, StopIteration) is not StopIteration]

        if mixer == "conv":
            carry_ref[...] = carry["prev"]
        for src, dst in zip(next_f32_refs, next_bf16_refs):
            dst[...] = src[...].astype(_MXU_DTYPE)

    tm, d = ROW_TILE, h_hbm.shape[1]
    tile = pl.BlockSpec((tm, d), lambda i: (i, 0))
    cast_in = [pl.BlockSpec((rows, cols), lambda i, per=per: (i // per, 0))
               for rows, cols, per in cast_blocks]
    pltpu.emit_pipeline(
        step, grid=(n_steps,),
        in_specs=[tile, pl.BlockSpec((tm, p_hbm.shape[-1]), lambda i: (i, 0))] + cast_in,
        out_specs=[tile] + cast_in,
        _explicit_indices=True,
    )(h_hbm, p_hbm.at[layer], *(r.at[k] for r, k in zip(next_f32_hbm, cast_idx)),
      o_hbm, *next_bf16_hbm)


def _row_blocks(rows, n_steps):
    n = n_steps
    while rows % (n * BF16_ROWS):
        assert n % 2 == 0, (rows, n_steps)
        n //= 2
    return n


def _run_layer(h, p, layer, mixer, mixer_layer, big, small, norms, final_norm, next_big, seq_len,
               scratch_shapes):
    t, d = h.shape
    tm = ROW_TILE
    assert t % tm == 0 and seq_len % tm == 0
    n_steps = t // tm
    is_last = final_norm is not None
    hbm = pl.BlockSpec(memory_space=pl.ANY)
    vmem = pl.BlockSpec(memory_space=pltpu.VMEM)

    operands, in_specs = [h, p], [hbm, hbm]
    for arr in norms:
        operands.append(arr)
        in_specs.append(vmem)
    scratch_shapes = list(scratch_shapes)
    own_layers = None
    if isinstance(big[0], tuple):
        own_layers = tuple(idx for _, idx in big)
        big = [arr for arr, _ in big]
        scratch_shapes += [pltpu.VMEM(arr.shape[1:], _MXU_DTYPE) for arr in big]
        scratch_shapes += [pltpu.VMEM((STAGE_SLOTS, STAGE_ROWS, STAGE_COLS), jnp.float32),
                           pltpu.SemaphoreType.DMA((STAGE_SLOTS,))]
    else:
        scratch_shapes += [pltpu.VMEM(arr.shape, _MXU_DTYPE) for arr in big]
    for arr in big:
        operands.append(arr)
        in_specs.append(hbm)
    n_small_rows = sum(arr.ndim == 2 for arr in small)
    assert all(arr.ndim == 2 for arr in small[:n_small_rows])
    for arr in small:
        operands.append(arr)
        in_specs.append(vmem)
    if is_last:
        operands.append(final_norm)
        in_specs.append(vmem)

    out_shape = [jax.ShapeDtypeStruct((t, d), jnp.float32)]
    cast_blocks = []
    for arr, idx in next_big:
        _, rows, cols = arr.shape
        n_blocks = _row_blocks(rows, n_steps)
        operands.append(arr)
        in_specs.append(hbm)
        out_shape.append(jax.ShapeDtypeStruct((rows, cols), _MXU_DTYPE))
        cast_blocks.append((rows // n_blocks, cols, n_steps // n_blocks))

    kernel = functools.partial(_layer_kernel, mixer=mixer, layer=layer, mixer_layer=mixer_layer,
                               n_small=len(small), n_small_rows=n_small_rows, is_last=is_last,
                               cast_idx=tuple(idx for _, idx in next_big),
                               cast_blocks=tuple(cast_blocks), n_steps=n_steps,
                               tiles_per_seq=seq_len // tm, own_layers=own_layers)
    outs = pl.pallas_call(
        kernel,
        in_specs=in_specs,
        out_specs=[hbm] * len(out_shape),
        out_shape=out_shape,
        scratch_shapes=scratch_shapes,
        compiler_params=pltpu.CompilerParams(vmem_limit_bytes=VMEM_LIMIT_BYTES),
        name=f"layer{layer}_{mixer}",
    )(*operands)
    return outs[0], list(outs[1:])


def kernel(x, p, mix_norm, conv_w_in, conv_w, conv_w_out, sg_w_in, sg_v_gain, sg_v_bias,
           sg_w_spatial, sg_b_spatial, sg_w_out, ffn_norm, ffn_w_gate, ffn_w_up, ffn_w_down,
           ple_norm, ple_w_gate, ple_w_proj, final_norm):
    bsz, seq_len, d = x.shape
    depth = p.shape[0]
    t = bsz * seq_len
    n_groups, chunk = sg_b_spatial.shape[1:]
    gd = d // n_groups
    rb = ROW_TILE // ROW_CHAINS
    assert ROW_TILE % ROW_CHAINS == 0 and rb % chunk == 0 and rb % BF16_ROWS == 0
    assert d % MXU_COLS == 0 and ffn_w_gate.shape[-1] % MXU_COLS == 0 and MXU_COLS % gd == 0
    assert d % n_groups == 0 and sg_w_spatial.shape[-2:] == (chunk, chunk)

    norms = (mix_norm, ffn_norm, ple_norm)
    b_s_tile = jnp.repeat(jnp.swapaxes(sg_b_spatial, 1, 2), gd, axis=2)

    def big_f32(layer):
        j = layer // 2
        mix = [(conv_w_in, j), (conv_w_out, j)] if layer % 2 == 0 else [(sg_w_in, j), (sg_w_out, j)]
        return mix + [(w, layer) for w in (ffn_w_gate, ffn_w_up, ffn_w_down, ple_w_gate, ple_w_proj)]

    h = x.reshape(t, d)
    p2 = p.reshape(depth, t, p.shape[-1])
    big = big_f32(0)
    for layer in range(depth):
        last = layer == depth - 1
        fn = final_norm.reshape(1, d) if last else None
        next_big = [] if last else big_f32(layer + 1)
        j = layer // 2
        if layer % 2 == 0:
            mixer, small = "conv", [conv_w]
            scratch = [pltpu.VMEM((SUBLANES, d), jnp.float32)]
        else:
            mixer = "sg"
            small = [sg_v_gain, sg_v_bias, sg_w_spatial, b_s_tile]
            scratch = [pltpu.VMEM((ROW_TILE, d), _MXU_DTYPE),
                       pltpu.VMEM((ROW_TILE, d), _MXU_DTYPE)]
        h, big = _run_layer(h, p2, layer, mixer, j, big, small, norms, fn, next_big, seq_len,
                            scratch)
    return h.reshape(bsz, seq_len, d)
```

```python
import functools

import jax
import jax.numpy as jnp
from jax import lax
from jax.experimental import pallas as pl
from jax.experimental.pallas import tpu as pltpu

RMS_EPS = 1e-6
LN_EPS = 1e-5
SUBLANES = 8
BF16_ROWS = 16
ROW_TILE = 512
ROW_CHAINS = 2
MXU_COLS = 256
STAGE_ROWS, STAGE_COLS = 256, 1024
STAGE_SLOTS = 6
COPY_SLOTS = 3
VMEM_LIMIT_BYTES = 58 * 1024 * 1024

_MXU_DTYPE = jnp.bfloat16
N_BIG = 7


def _rms(x, g):
    var = jnp.mean(x * x, axis=-1, keepdims=True)
    return (x * lax.rsqrt(var + RMS_EPS)) * g


def _dot(a, b):
    return jnp.dot(a, b, preferred_element_type=jnp.float32)


def _conv_gate(b_gate, z, prev, w_conv_ref):
    tm, d = z.shape

    w = w_conv_ref[...]
    w0, w1, w2 = w[0:1, :], w[1:2, :], w[2:3, :]
    y = w2 * z + w1 * pltpu.roll(z, 1, axis=0) + w0 * pltpu.roll(z, 2, axis=0)
    row = lax.broadcasted_iota(jnp.int32, (SUBLANES, d), 0)
    zh = z[:SUBLANES, :]
    z1h = jnp.where(row < 1, pltpu.roll(prev, 1, axis=0), pltpu.roll(zh, 1, axis=0))
    z2h = jnp.where(row < 2, pltpu.roll(prev, 2, axis=0), pltpu.roll(zh, 2, axis=0))
    yh = w2 * zh + w1 * z1h + w0 * z2h
    y = jnp.concatenate([yh, y[SUBLANES:, :]], axis=0)
    return (b_gate * y).astype(_MXU_DTYPE), z[tm - SUBLANES:, :]


def _sg_gate(hn, w_in_ref, v, gain_ref, bias_ref, w_s_ref, b_s_ref, v_ref, y_ref):
    tm, d = v.shape
    n_groups, chunk, _ = w_s_ref.shape
    gd = d // n_groups
    n_chunks = tm // chunk
    mu = jnp.mean(v, axis=-1, keepdims=True)
    vc = v - mu
    var = jnp.mean(vc * vc, axis=-1, keepdims=True)
    v = (vc * lax.rsqrt(var + LN_EPS)) * gain_ref[...] + bias_ref[...]
    v_ref[...] = v.astype(_MXU_DTYPE)

    t_idx = lax.broadcasted_iota(jnp.int32, (chunk, chunk), 0)
    s_idx = lax.broadcasted_iota(jnp.int32, (chunk, chunk), 1)
    causal = s_idx <= t_idx
    b_s = b_s_ref[...]
    for g in range(n_groups):
        if g * gd % MXU_COLS == 0:
            u_cols = g * gd
            u = _dot(hn, w_in_ref[:, u_cols:u_cols + MXU_COLS])
        w_g = jnp.where(causal, w_s_ref[g], 0.0).astype(_MXU_DTYPE)
        cols = slice(g * gd, (g + 1) * gd)
        ucols = slice(g * gd - u_cols, (g + 1) * gd - u_cols)
        rhs = jnp.concatenate(
            [v_ref[c * chunk:(c + 1) * chunk, cols] for c in range(n_chunks)], axis=1)
        mixed = _dot(w_g, rhs)
        for c in range(n_chunks):
            rows = slice(c * chunk, (c + 1) * chunk)
            y_ref[rows, cols] = (
                u[rows, ucols] * (mixed[:, c * gd:(c + 1) * gd] + b_s[:, cols])
            ).astype(_MXU_DTYPE)
    return y_ref[...]


class _WeightFeed:
    def __init__(self, hbm_refs, layers, vmem_refs, stage_ref, sem_ref):
        self.hbm_refs, self.layers, self.vmem_refs = hbm_refs, layers, vmem_refs
        self.stage_ref, self.sem_ref = stage_ref, sem_ref
        self.n_slots = sem_ref.shape[0]
        self.chunks = []
        for w, dst in enumerate(vmem_refs):
            k, n = dst.shape
            if layers is None:
                self.chunks.append((w, 0, k, 0, n))
            else:
                assert k % STAGE_ROWS == 0
                self.chunks += [(w, r0, STAGE_ROWS, c0, min(STAGE_COLS, n - c0))
                                for r0 in range(0, k, STAGE_ROWS)
                                for c0 in range(0, n, STAGE_COLS)]
        self.done = 0
        for j in range(min(self.n_slots - 1, len(self.chunks))):
            self._copy(j).start()

    def _copy(self, j):
        w, r0, rows, c0, cols = self.chunks[j]
        slot = j % self.n_slots
        if self.layers is None:
            src, dst = self.hbm_refs[w], self.vmem_refs[w]
        else:
            src = self.hbm_refs[w].at[self.layers[w], pl.ds(r0, rows), pl.ds(c0, cols)]
            dst = self.stage_ref.at[slot, :, pl.ds(0, cols)]
        return pltpu.make_async_copy(src, dst, self.sem_ref.at[slot])

    def ensure(self, weight):
        while self.done < len(self.chunks) and self.chunks[self.done][0] <= weight:
            j = self.done
            w, r0, rows, c0, cols = self.chunks[j]
            ahead = j + self.n_slots - 1
            if ahead < len(self.chunks):
                self._copy(ahead).start()
            self._copy(j).wait()
            if self.layers is not None:
                self.vmem_refs[w][pl.ds(r0, rows), pl.ds(c0, cols)] = (
                    self.stage_ref[j % self.n_slots, :, pl.ds(0, cols)].astype(_MXU_DTYPE))
            self.done += 1


W_IN, W_OUT, W_GATE, W_UP, W_DOWN, PLE_W_GATE, PLE_W_PROJ = range(N_BIG)


def _layer_kernel(*refs, mixer, layer, mixer_layer, n_small, n_small_rows, is_last, n_next,
                  tiles_per_seq, own_layers):
    refs = list(refs)
    take = lambda n: [refs.pop(0) for _ in range(n)]
    h_ref, p_ref = take(2)
    mix_norm_ref, ffn_norm_ref, ple_norm_ref = (r.at[pl.ds(layer, 1), :] for r in take(3))
    hbm_w_refs = take(N_BIG)
    small_refs = take(n_small)
    small_refs[:n_small_rows] = [r.at[pl.ds(mixer_layer, 1), :]
                                 for r in small_refs[:n_small_rows]]
    final_norm_ref = take(1)[0] if is_last else None
    next_f32_refs = take(n_next)
    o_ref = take(1)[0]
    next_bf16_refs = take(n_next)
    sem_ref = refs.pop()
    stage_ref = refs.pop() if own_layers is not None else None
    scratch, w_refs = refs[:-N_BIG], refs[-N_BIG:]
    (w_in_ref, w_out_ref, w_gate_ref, w_up_ref, w_down_ref,
     ple_w_gate_ref, ple_w_proj_ref) = w_refs

    def step(feed):
        need = feed.ensure if feed is not None else lambda weight: None
        carry = {}
        if mixer == "conv":
            carry_ref, = scratch

            @pl.when(pl.program_id(0) % tiles_per_seq == 0)
            def _():
                carry_ref[...] = jnp.zeros_like(carry_ref)

            carry["prev"] = carry_ref[...]

        def row_block(rows):
            h = h_ref[rows, :]
            hn = _rms(h, mix_norm_ref[...]).astype(_MXU_DTYPE)
            d = h.shape[1]
            need(W_IN)
            if mixer == "conv":
                b_gate = _dot(hn, w_in_ref[:, :d])
                z = jnp.concatenate(
                    [_dot(hn, w_in_ref[:, d + n:d + n + MXU_COLS])
                     * _dot(hn, w_in_ref[:, 2 * d + n:2 * d + n + MXU_COLS])
                     for n in range(0, d, MXU_COLS)], axis=1)
            else:
                v = _dot(hn, w_in_ref[:, d:])
            yield
            if mixer == "conv":
                gated, carry["prev"] = _conv_gate(b_gate, z, carry["prev"], *small_refs)
            else:
                gated = _sg_gate(hn, w_in_ref, v, *small_refs,
                                 *(s.at[rows, :] for s in scratch))
            need(W_OUT)
            mix = _dot(gated, w_out_ref[...])
            yield
            h = h + mix
            hn = _rms(h, ffn_norm_ref[...]).astype(_MXU_DTYPE)
            need(W_UP)
            acts = []
            for n in range(w_gate_ref.shape[1] // MXU_COLS):
                cols = slice(n * MXU_COLS, (n + 1) * MXU_COLS)
                ffn_gate = _dot(hn, w_gate_ref[:, cols])
                ffn_up = _dot(hn, w_up_ref[:, cols])
                acts.append((jax.nn.silu(ffn_gate) * ffn_up).astype(_MXU_DTYPE))
            act = jnp.concatenate(acts, axis=1)
            yield
            need(W_DOWN)
            ffn = _dot(act, w_down_ref[...])
            yield
            h = h + ffn
            hn = _rms(h, ple_norm_ref[...]).astype(_MXU_DTYPE)
            pb = p_ref[rows, :].astype(_MXU_DTYPE)
            need(PLE_W_PROJ)
            h = jnp.concatenate(
                [h[:, n:n + MXU_COLS]
                 + jax.nn.sigmoid(_dot(hn, ple_w_gate_ref[:, n:n + MXU_COLS]))
                 * _dot(pb, ple_w_proj_ref[:, n:n + MXU_COLS])
                 for n in range(0, d, MXU_COLS)], axis=1)
            yield
            if is_last:
                h = _rms(h, final_norm_ref[...])
            o_ref[rows, :] = h

        rb = h_ref.shape[0] // ROW_CHAINS
        blocks = [row_block(pl.ds(c * rb, rb)) for c in range(ROW_CHAINS)]
        while blocks:
            blocks = [b for b in blocks if next(b, StopIteration) is not StopIteration]

        if mixer == "conv":
            carry_ref[...] = carry["prev"]
        for src, dst in zip(next_f32_refs, next_bf16_refs):
            dst[...] = src[...].astype(_MXU_DTYPE)

    if mixer == "conv":
        @pl.when(pl.program_id(0) == 0)
        def _():
            _WeightFeed(hbm_w_refs, own_layers, w_refs, stage_ref, sem_ref).ensure(N_BIG - 1)

        step(None)
    else:
        @pl.when(pl.program_id(0) == 0)
        def _():
            feed = _WeightFeed(hbm_w_refs, own_layers, w_refs, stage_ref, sem_ref)
            step(feed)
            feed.ensure(N_BIG - 1)

        @pl.when(pl.program_id(0) > 0)
        def _():
            step(None)


def _const_spec(shape, index):
    return pl.BlockSpec(shape, lambda i: index, pipeline_mode=pl.Buffered(1))


def _stacked(arr, layer):
    tail = arr.shape[1:]
    return _const_spec((None,) + tail, (layer,) + (0,) * len(tail))


def _row_blocks(rows, n_steps):
    n = n_steps
    while rows % (n * BF16_ROWS):
        assert n % 2 == 0, (rows, n_steps)
        n //= 2
    return n


def _run_layer(h, p, layer, mixer, mixer_layer, big, small, norms, final_norm, next_big, seq_len,
               scratch_shapes):
    t, d = h.shape
    tm = ROW_TILE
    assert t % tm == 0 and seq_len % tm == 0
    n_steps = t // tm
    is_last = final_norm is not None

    operands = [h, p]
    in_specs = [pl.BlockSpec((tm, d), lambda i: (i, 0)),
                pl.BlockSpec((None, tm, p.shape[-1]), lambda i: (layer, i, 0))]
    for arr in norms:
        operands.append(arr)
        in_specs.append(_const_spec(arr.shape, (0, 0)))
    scratch_shapes = list(scratch_shapes)
    if isinstance(big[0], tuple):
        own_layers = tuple(idx for _, idx in big)
        big = [arr for arr, _ in big]
        scratch_shapes += [pltpu.VMEM(arr.shape[1:], _MXU_DTYPE) for arr in big]
        scratch_shapes += [pltpu.VMEM((STAGE_SLOTS, STAGE_ROWS, STAGE_COLS), jnp.float32),
                           pltpu.SemaphoreType.DMA((STAGE_SLOTS,))]
    else:
        own_layers = None
        scratch_shapes += [pltpu.VMEM(arr.shape, _MXU_DTYPE) for arr in big]
        scratch_shapes += [pltpu.SemaphoreType.DMA((COPY_SLOTS,))]
    for arr in big:
        operands.append(arr)
        in_specs.append(pl.BlockSpec(memory_space=pl.ANY))
    n_small_rows = sum(arr.ndim == 2 for arr in small)
    assert all(arr.ndim == 2 for arr in small[:n_small_rows])
    for arr in small:
        operands.append(arr)
        in_specs.append(_const_spec(arr.shape, (0, 0)) if arr.ndim == 2
                        else _stacked(arr, mixer_layer))
    if is_last:
        operands.append(final_norm)
        in_specs.append(_const_spec(final_norm.shape, (0, 0)))

    out_shape = [jax.ShapeDtypeStruct((t, d), jnp.float32)]
    out_specs = [pl.BlockSpec((tm, d), lambda i: (i, 0))]
    for arr, idx in next_big:
        _, rows, cols = arr.shape
        n_blocks = _row_blocks(rows, n_steps)
        per = n_steps // n_blocks
        operands.append(arr)
        in_specs.append(pl.BlockSpec((None, rows // n_blocks, cols),
                                     lambda i, idx=idx, per=per: (idx, i // per, 0)))
        out_shape.append(jax.ShapeDtypeStruct((rows, cols), _MXU_DTYPE))
        out_specs.append(pl.BlockSpec((rows // n_blocks, cols),
                                      lambda i, per=per: (i // per, 0)))

    kernel = functools.partial(_layer_kernel, mixer=mixer, layer=layer, mixer_layer=mixer_layer,
                               n_small=len(small), n_small_rows=n_small_rows, is_last=is_last,
                               n_next=len(next_big), tiles_per_seq=seq_len // tm,
                               own_layers=own_layers)
    outs = pl.pallas_call(
        kernel,
        grid=(n_steps,),
        in_specs=in_specs,
        out_specs=out_specs,
        out_shape=out_shape,
        scratch_shapes=scratch_shapes,
        compiler_params=pltpu.CompilerParams(
            dimension_semantics=("arbitrary",),
            vmem_limit_bytes=VMEM_LIMIT_BYTES),
        name=f"layer{layer}_{mixer}",
    )(*operands)
    return outs[0], list(outs[1:])


def kernel(x, p, mix_norm, conv_w_in, conv_w, conv_w_out, sg_w_in, sg_v_gain, sg_v_bias,
           sg_w_spatial, sg_b_spatial, sg_w_out, ffn_norm, ffn_w_gate, ffn_w_up, ffn_w_down,
           ple_norm, ple_w_gate, ple_w_proj, final_norm):
    bsz, seq_len, d = x.shape
    depth = p.shape[0]
    t = bsz * seq_len
    n_groups, chunk = sg_b_spatial.shape[1:]
    gd = d // n_groups
    rb = ROW_TILE // ROW_CHAINS
    assert ROW_TILE % ROW_CHAINS == 0 and rb % chunk == 0 and rb % BF16_ROWS == 0
    assert d % MXU_COLS == 0 and ffn_w_gate.shape[-1] % MXU_COLS == 0 and MXU_COLS % gd == 0
    assert d % n_groups == 0 and sg_w_spatial.shape[-2:] == (chunk, chunk)

    norms = (mix_norm, ffn_norm, ple_norm)
    b_s_tile = jnp.repeat(jnp.swapaxes(sg_b_spatial, 1, 2), gd, axis=2)

    def big_f32(layer):
        j = layer // 2
        mix = [(conv_w_in, j), (conv_w_out, j)] if layer % 2 == 0 else [(sg_w_in, j), (sg_w_out, j)]
        return mix + [(w, layer) for w in (ffn_w_gate, ffn_w_up, ffn_w_down, ple_w_gate, ple_w_proj)]

    h = x.reshape(t, d)
    p2 = p.reshape(depth, t, p.shape[-1])
    big = big_f32(0)
    for layer in range(depth):
        last = layer == depth - 1
        fn = final_norm.reshape(1, d) if last else None
        next_big = [] if last else big_f32(layer + 1)
        j = layer // 2
        if layer % 2 == 0:
            mixer, small = "conv", [conv_w]
            scratch = [pltpu.VMEM((SUBLANES, d), jnp.float32)]
        else:
            mixer = "sg"
            small = [sg_v_gain, sg_v_bias, sg_w_spatial, b_s_tile]
            scratch = [pltpu.VMEM((ROW_TILE, d), _MXU_DTYPE),
                       pltpu.VMEM((ROW_TILE, d), _MXU_DTYPE)]
        h, big = _run_layer(h, p2, layer, mixer, j, big, small, norms, fn, next_big, seq_len,
                            scratch)
    return h.reshape(bsz, seq_len, d)
```

```python
import functools

import jax
import jax.numpy as jnp
from jax import lax
from jax.experimental import pallas as pl
from jax.experimental.pallas import tpu as pltpu

RMS_EPS = 1e-6
LN_EPS = 1e-5
SUBLANES = 8
BF16_ROWS = 16
ROW_TILE = 512
ROW_CHAINS = 2
MXU_COLS = 256
STAGE_ROWS, STAGE_COLS = 256, 1024
STAGE_SLOTS = 6
VMEM_LIMIT_BYTES = 58 * 1024 * 1024

_MXU_DTYPE = jnp.bfloat16
N_BIG = 7


def _rms(x, g):
    var = jnp.mean(x * x, axis=-1, keepdims=True)
    return (x * lax.rsqrt(var + RMS_EPS)) * g


def _dot(a, b):
    return jnp.dot(a, b, preferred_element_type=jnp.float32)


def _conv_gate(b_gate, z, prev, w_conv_ref):
    tm, d = z.shape

    w = w_conv_ref[...]
    w0, w1, w2 = w[0:1, :], w[1:2, :], w[2:3, :]
    y = w2 * z + w1 * pltpu.roll(z, 1, axis=0) + w0 * pltpu.roll(z, 2, axis=0)
    row = lax.broadcasted_iota(jnp.int32, (SUBLANES, d), 0)
    zh = z[:SUBLANES, :]
    z1h = jnp.where(row < 1, pltpu.roll(prev, 1, axis=0), pltpu.roll(zh, 1, axis=0))
    z2h = jnp.where(row < 2, pltpu.roll(prev, 2, axis=0), pltpu.roll(zh, 2, axis=0))
    yh = w2 * zh + w1 * z1h + w0 * z2h
    y = jnp.concatenate([yh, y[SUBLANES:, :]], axis=0)
    return (b_gate * y).astype(_MXU_DTYPE), z[tm - SUBLANES:, :]


def _sg_gate(hn, w_in_ref, v, gain_ref, bias_ref, w_s_ref, b_s_ref, v_ref, y_ref):
    tm, d = v.shape
    n_groups, chunk, _ = w_s_ref.shape
    gd = d // n_groups
    n_chunks = tm // chunk
    mu = jnp.mean(v, axis=-1, keepdims=True)
    vc = v - mu
    var = jnp.mean(vc * vc, axis=-1, keepdims=True)
    v = (vc * lax.rsqrt(var + LN_EPS)) * gain_ref[...] + bias_ref[...]
    v_ref[...] = v.astype(_MXU_DTYPE)

    t_idx = lax.broadcasted_iota(jnp.int32, (chunk, chunk), 0)
    s_idx = lax.broadcasted_iota(jnp.int32, (chunk, chunk), 1)
    causal = s_idx <= t_idx
    b_s = b_s_ref[...]
    for g in range(n_groups):
        if g * gd % MXU_COLS == 0:
            u_cols = g * gd
            u = _dot(hn, w_in_ref[:, u_cols:u_cols + MXU_COLS])
        w_g = jnp.where(causal, w_s_ref[g], 0.0).astype(_MXU_DTYPE)
        cols = slice(g * gd, (g + 1) * gd)
        ucols = slice(g * gd - u_cols, (g + 1) * gd - u_cols)
        rhs = jnp.concatenate(
            [v_ref[c * chunk:(c + 1) * chunk, cols] for c in range(n_chunks)], axis=1)
        mixed = _dot(w_g, rhs)
        for c in range(n_chunks):
            rows = slice(c * chunk, (c + 1) * chunk)
            y_ref[rows, cols] = (
                u[rows, ucols] * (mixed[:, c * gd:(c + 1) * gd] + b_s[:, cols])
            ).astype(_MXU_DTYPE)
    return y_ref[...]


def _round_weights_into_vmem(hbm_refs, layers, vmem_refs, stage_ref, sem_ref):
    n_slots = stage_ref.shape[0]
    chunks = []
    for w, dst in enumerate(vmem_refs):
        k, n = dst.shape
        assert k % STAGE_ROWS == 0
        chunks += [(w, r0, c0, min(STAGE_COLS, n - c0))
                   for r0 in range(0, k, STAGE_ROWS) for c0 in range(0, n, STAGE_COLS)]

    def copy(j):
        w, r0, c0, cols = chunks[j]
        return pltpu.make_async_copy(
            hbm_refs[w].at[layers[w], pl.ds(r0, STAGE_ROWS), pl.ds(c0, cols)],
            stage_ref.at[j % n_slots, :, pl.ds(0, cols)],
            sem_ref.at[j % n_slots])

    for j in range(min(n_slots - 1, len(chunks))):
        copy(j).start(priority=j % 2)
    for j, (w, r0, c0, cols) in enumerate(chunks):
        ahead = j + n_slots - 1
        if ahead < len(chunks):
            copy(ahead).start(priority=ahead % 2)
        copy(j).wait()
        vmem_refs[w][pl.ds(r0, STAGE_ROWS), pl.ds(c0, cols)] = (
            stage_ref[j % n_slots, :, pl.ds(0, cols)].astype(_MXU_DTYPE))


def _layer_kernel(*refs, mixer, layer, mixer_layer, n_small, n_small_rows, is_last, n_next,
                  tiles_per_seq, own_layers):
    refs = list(refs)
    take = lambda n: [refs.pop(0) for _ in range(n)]
    h_ref, p_ref = take(2)
    mix_norm_ref, ffn_norm_ref, ple_norm_ref = (r.at[pl.ds(layer, 1), :] for r in take(3))
    big_refs = take(N_BIG)
    small_refs = take(n_small)
    small_refs[:n_small_rows] = [r.at[pl.ds(mixer_layer, 1), :]
                                 for r in small_refs[:n_small_rows]]
    final_norm_ref = take(1)[0] if is_last else None
    next_f32_refs = take(n_next)
    o_ref = take(1)[0]
    next_bf16_refs = take(n_next)
    scratch = refs

    if own_layers is not None:
        w_vmem, (stage_ref, sem_ref) = scratch[-N_BIG - 2:-2], scratch[-2:]
        scratch = scratch[:-N_BIG - 2]

        @pl.when(pl.program_id(0) == 0)
        def _():
            _round_weights_into_vmem(big_refs, own_layers, w_vmem, stage_ref, sem_ref)

        big_refs = w_vmem
    (w_in_ref, w_out_ref, w_gate_ref, w_up_ref, w_down_ref,
     ple_w_gate_ref, ple_w_proj_ref) = big_refs

    carry = {}
    if mixer == "conv":
        carry_ref, = scratch

        @pl.when(pl.program_id(0) % tiles_per_seq == 0)
        def _():
            carry_ref[...] = jnp.zeros_like(carry_ref)

        carry["prev"] = carry_ref[...]

    def row_block(rows):
        h = h_ref[rows, :]
        hn = _rms(h, mix_norm_ref[...]).astype(_MXU_DTYPE)
        d = h.shape[1]
        if mixer == "conv":
            b_gate = _dot(hn, w_in_ref[:, :d])
            z = jnp.concatenate(
                [_dot(hn, w_in_ref[:, d + n:d + n + MXU_COLS])
                 * _dot(hn, w_in_ref[:, 2 * d + n:2 * d + n + MXU_COLS])
                 for n in range(0, d, MXU_COLS)], axis=1)
        else:
            v = _dot(hn, w_in_ref[:, d:])
        yield
        if mixer == "conv":
            gated, carry["prev"] = _conv_gate(b_gate, z, carry["prev"], *small_refs)
        else:
            gated = _sg_gate(hn, w_in_ref, v, *small_refs, *(s.at[rows, :] for s in scratch))
        mix = _dot(gated, w_out_ref[...])
        yield
        h = h + mix
        hn = _rms(h, ffn_norm_ref[...]).astype(_MXU_DTYPE)
        acts = []
        for n in range(w_gate_ref.shape[1] // MXU_COLS):
            cols = slice(n * MXU_COLS, (n + 1) * MXU_COLS)
            ffn_gate = _dot(hn, w_gate_ref[:, cols])
            ffn_up = _dot(hn, w_up_ref[:, cols])
            acts.append((jax.nn.silu(ffn_gate) * ffn_up).astype(_MXU_DTYPE))
        act = jnp.concatenate(acts, axis=1)
        yield
        ffn = _dot(act, w_down_ref[...])
        yield
        h = h + ffn
        hn = _rms(h, ple_norm_ref[...]).astype(_MXU_DTYPE)
        pb = p_ref[rows, :].astype(_MXU_DTYPE)
        h = jnp.concatenate(
            [h[:, n:n + MXU_COLS]
             + jax.nn.sigmoid(_dot(hn, ple_w_gate_ref[:, n:n + MXU_COLS]))
             * _dot(pb, ple_w_proj_ref[:, n:n + MXU_COLS])
             for n in range(0, d, MXU_COLS)], axis=1)
        yield
        if is_last:
            h = _rms(h, final_norm_ref[...])
        o_ref[rows, :] = h

    rb = h_ref.shape[0] // ROW_CHAINS
    blocks = [row_block(pl.ds(c * rb, rb)) for c in range(ROW_CHAINS)]
    while blocks:
        blocks = [b for b in blocks if next(b, StopIteration) is not StopIteration]

    if mixer == "conv":
        carry_ref[...] = carry["prev"]
    for src, dst in zip(next_f32_refs, next_bf16_refs):
        dst[...] = src[...].astype(_MXU_DTYPE)


def _const_spec(shape, index):
    return pl.BlockSpec(shape, lambda i: index, pipeline_mode=pl.Buffered(1))


def _stacked(arr, layer):
    tail = arr.shape[1:]
    return _const_spec((None,) + tail, (layer,) + (0,) * len(tail))


def _row_blocks(rows, n_steps):
    n = n_steps
    while rows % (n * BF16_ROWS):
        assert n % 2 == 0, (rows, n_steps)
        n //= 2
    return n


def _run_layer(h, p, layer, mixer, mixer_layer, big, small, norms, final_norm, next_big, seq_len,
               scratch_shapes):
    t, d = h.shape
    tm = ROW_TILE
    assert t % tm == 0 and seq_len % tm == 0
    n_steps = t // tm
    is_last = final_norm is not None

    operands = [h, p]
    in_specs = [pl.BlockSpec((tm, d), lambda i: (i, 0)),
                pl.BlockSpec((None, tm, p.shape[-1]), lambda i: (layer, i, 0))]
    for arr in norms:
        operands.append(arr)
        in_specs.append(_const_spec(arr.shape, (0, 0)))
    own_layers = None
    if isinstance(big[0], tuple):
        own_layers = tuple(idx for _, idx in big)
        scratch_shapes = list(scratch_shapes)
        scratch_shapes += [pltpu.VMEM(arr.shape[1:], _MXU_DTYPE) for arr, _ in big]
        scratch_shapes += [pltpu.VMEM((STAGE_SLOTS, STAGE_ROWS, STAGE_COLS), jnp.float32),
                           pltpu.SemaphoreType.DMA((STAGE_SLOTS,))]
        for arr, _ in big:
            operands.append(arr)
            in_specs.append(pl.BlockSpec(memory_space=pl.ANY))
    else:
        for arr in big:
            operands.append(arr)
            in_specs.append(_const_spec(arr.shape, (0, 0)))
    n_small_rows = sum(arr.ndim == 2 for arr in small)
    assert all(arr.ndim == 2 for arr in small[:n_small_rows])
    for arr in small:
        operands.append(arr)
        in_specs.append(_const_spec(arr.shape, (0, 0)) if arr.ndim == 2
                        else _stacked(arr, mixer_layer))
    if is_last:
        operands.append(final_norm)
        in_specs.append(_const_spec(final_norm.shape, (0, 0)))

    out_shape = [jax.ShapeDtypeStruct((t, d), jnp.float32)]
    out_specs = [pl.BlockSpec((tm, d), lambda i: (i, 0))]
    for arr, idx in next_big:
        _, rows, cols = arr.shape
        n_blocks = _row_blocks(rows, n_steps)
        per = n_steps // n_blocks
        operands.append(arr)
        in_specs.append(pl.BlockSpec((None, rows // n_blocks, cols),
                                     lambda i, idx=idx, per=per: (idx, i // per, 0)))
        out_shape.append(jax.ShapeDtypeStruct((rows, cols), _MXU_DTYPE))
        out_specs.append(pl.BlockSpec((rows // n_blocks, cols),
                                      lambda i, per=per: (i // per, 0)))

    kernel = functools.partial(_layer_kernel, mixer=mixer, layer=layer, mixer_layer=mixer_layer,
                               n_small=len(small), n_small_rows=n_small_rows, is_last=is_last,
                               n_next=len(next_big), tiles_per_seq=seq_len // tm,
                               own_layers=own_layers)
    outs = pl.pallas_call(
        kernel,
        grid=(n_steps,),
        in_specs=in_specs,
        out_specs=out_specs,
        out_shape=out_shape,
        scratch_shapes=scratch_shapes,
        compiler_params=pltpu.CompilerParams(
            dimension_semantics=("arbitrary",),
            vmem_limit_bytes=VMEM_LIMIT_BYTES),
        name=f"layer{layer}_{mixer}",
    )(*operands)
    return outs[0], list(outs[1:])


def kernel(x, p, mix_norm, conv_w_in, conv_w, conv_w_out, sg_w_in, sg_v_gain, sg_v_bias,
           sg_w_spatial, sg_b_spatial, sg_w_out, ffn_norm, ffn_w_gate, ffn_w_up, ffn_w_down,
           ple_norm, ple_w_gate, ple_w_proj, final_norm):
    bsz, seq_len, d = x.shape
    depth = p.shape[0]
    t = bsz * seq_len
    n_groups, chunk = sg_b_spatial.shape[1:]
    gd = d // n_groups
    rb = ROW_TILE // ROW_CHAINS
    assert ROW_TILE % ROW_CHAINS == 0 and rb % chunk == 0 and rb % BF16_ROWS == 0
    assert d % MXU_COLS == 0 and ffn_w_gate.shape[-1] % MXU_COLS == 0 and MXU_COLS % gd == 0
    assert d % n_groups == 0 and sg_w_spatial.shape[-2:] == (chunk, chunk)

    norms = (mix_norm, ffn_norm, ple_norm)
    b_s_tile = jnp.repeat(jnp.swapaxes(sg_b_spatial, 1, 2), gd, axis=2)

    def big_f32(layer):
        j = layer // 2
        mix = [(conv_w_in, j), (conv_w_out, j)] if layer % 2 == 0 else [(sg_w_in, j), (sg_w_out, j)]
        return mix + [(w, layer) for w in (ffn_w_gate, ffn_w_up, ffn_w_down, ple_w_gate, ple_w_proj)]

    h = x.reshape(t, d)
    p2 = p.reshape(depth, t, p.shape[-1])
    big = big_f32(0)
    for layer in range(depth):
        last = layer == depth - 1
        fn = final_norm.reshape(1, d) if last else None
        next_big = [] if last else big_f32(layer + 1)
        j = layer // 2
        if layer % 2 == 0:
            mixer, small = "conv", [conv_w]
            scratch = [pltpu.VMEM((SUBLANES, d), jnp.float32)]
        else:
            mixer = "sg"
            small = [sg_v_gain, sg_v_bias, sg_w_spatial, b_s_tile]
            scratch = [pltpu.VMEM((ROW_TILE, d), _MXU_DTYPE),
                       pltpu.VMEM((ROW_TILE, d), _MXU_DTYPE)]
        h, big = _run_layer(h, p2, layer, mixer, j, big, small, norms, fn, next_big, seq_len,
                            scratch)
    return h.reshape(bsz, seq_len, d)
```
